```python
import math
import jax
import jax.numpy as jnp
from jax import lax
import numpy as np

D_MODEL = 1024
BATCH = 2
SEQ = 8192
DEPTH = 2

ATT_HEADS = 8
ATT_HEAD_DIM = 128
KV_LATENT = 256
IDX_HEADS = 8
IDX_DIM = 64
TOPK_MAX = 256
Q_BLOCK = 128
ATT_SCALE = ATT_HEAD_DIM ** -0.5
INDEX_SCALE = (IDX_HEADS ** -0.5) * (IDX_DIM ** -0.5)
DN_HEADS = 8
DN_DK = 128
DN_DV = 128
CONV_WIDTH = 4
CHUNK = 64
N_GROUPS = 4
EXPERTS_PER_GROUP = 8
N_EXPERTS = N_GROUPS * EXPERTS_PER_GROUP
EXPERT_TOPK = 2
EXPERT_FF = 512
MOE_BLOCK = 128
LN_EPS = 1e-5
RMS_EPS = 1e-6

ATT_Q_COLS = ATT_HEADS * ATT_HEAD_DIM
IDX_Q_COLS = IDX_HEADS * IDX_DIM
DN_QK_COLS = DN_HEADS * DN_DK
DN_V_COLS = DN_HEADS * DN_DV
CONV_CH = 2 * DN_QK_COLS + DN_V_COLS
COL_SIZES = (ATT_Q_COLS, KV_LATENT, IDX_Q_COLS, IDX_DIM, IDX_HEADS,
             CONV_CH, DN_HEADS, DN_HEADS, DN_V_COLS, 2 * D_MODEL)
N_IN = sum(COL_SIZES)
SPLITS = tuple(int(v) for v in np.cumsum(COL_SIZES)[:-1])

kernel_name = 'hybrid_dsa_gated_deltanet_hier_moe'


def layer_norm(x, g, b):
    xf = x.astype(jnp.float32)
    mu = jnp.mean(xf, axis=-1, keepdims=True)
    var = jnp.mean(jnp.square(xf - mu), axis=-1, keepdims=True)
    return ((xf - mu) * lax.rsqrt(var + LN_EPS)).astype(x.dtype) * g + b


def rms_norm(x, g):
    xf = x.astype(jnp.float32)
    return (xf * lax.rsqrt(jnp.mean(xf * xf, axis=-1, keepdims=True) + RMS_EPS)).astype(x.dtype) * g


def l2_normalize(x):
    return x * lax.rsqrt(jnp.sum(x * x, axis=-1, keepdims=True) + RMS_EPS)


def causal_conv(x, w):
    width = w.shape[0]
    return lax.conv_general_dilated(
        x, w[:, None, :], window_strides=(1,), padding=[(width - 1, 0)],
        dimension_numbers=('NWC', 'WIO', 'NWC'), feature_group_count=x.shape[-1])


def dsa_attention(q_att, c_kv, q_idx, k_idx, w_idx, w_uk, w_uv):
    B, S = c_kv.shape[0], c_kv.shape[1]
    topk = min(TOPK_MAX, S // 4)
    nb = S // Q_BLOCK
    q_lat = jnp.einsum('bshd,hcd->bshc', q_att, w_uk)
    key_pos = jnp.arange(S)
    gather = jax.vmap(lambda table, idx: table[idx])

    def to_blocks(t):
        return jnp.moveaxis(t.reshape((B, nb, Q_BLOCK) + t.shape[2:]), 1, 0)

    def block(args):
        ql, qi, wi, t0 = args
        qpos = t0 + jnp.arange(Q_BLOCK)
        rel = jax.nn.relu(jnp.einsum('bqhd,bsd->bqhs', qi, k_idx))
        iscore = jnp.einsum('bqhs,bqh->bqs', rel, wi).astype(jnp.float32)
        causal = key_pos[None, :] <= qpos[:, None]
        iscore = jnp.where(causal[None], iscore, -jnp.inf)
        _, sel = lax.top_k(iscore, topk)
        kv = gather(c_kv, sel)
        s = jnp.einsum('bqhc,bqkc->bqhk', ql, kv).astype(jnp.float32) * ATT_SCALE
        valid = sel <= qpos[None, :, None]
        s = jnp.where(valid[:, :, None, :], s, -jnp.inf)
        p = jax.nn.softmax(s, axis=-1).astype(kv.dtype)
        return jnp.einsum('bqhk,bqkc->bqhc', p, kv)

    o_lat = lax.map(block, (to_blocks(q_lat), to_blocks(q_idx), to_blocks(w_idx),
                            jnp.arange(nb) * Q_BLOCK))
    o_lat = jnp.moveaxis(o_lat, 0, 1).reshape(B, S, ATT_HEADS, KV_LATENT)
    o = jnp.einsum('bshc,hcd->bshd', o_lat, w_uv)
    return o.reshape(B, S, ATT_HEADS * ATT_HEAD_DIM)


def chunk_gated_delta_rule(q, k, v, beta, g):
    B, S, H, Dk = q.shape
    Dv = v.shape[-1]
    n = S // CHUNK

    def chunks(t):
        return t.reshape(B, n, CHUNK, H, -1).transpose(0, 3, 1, 2, 4)

    q, k, v = chunks(q), chunks(k), chunks(v)
    beta = beta.reshape(B, n, CHUNK, H).transpose(0, 3, 1, 2)
    G = jnp.cumsum(g.reshape(B, n, CHUNK, H).transpose(0, 3, 1, 2), axis=-1)
    tril = jnp.tril(jnp.ones((CHUNK, CHUNK), dtype=bool))
    tril_strict = jnp.tril(jnp.ones((CHUNK, CHUNK), dtype=bool), -1)
    decay = jnp.exp(jnp.where(tril, G[..., :, None] - G[..., None, :], -jnp.inf))
    kb = k * beta[..., None]
    A = jnp.where(tril_strict, jnp.einsum('bhncd,bhnsd->bhncs', kb, k) * decay, 0.0)
    eye = jnp.eye(CHUNK, dtype=A.dtype)
    rhs = jnp.concatenate([v * beta[..., None], kb * jnp.exp(G)[..., None]], axis=-1)
    sol = lax.linalg.triangular_solve(eye + A, rhs, left_side=True, lower=True,
                                      unit_diagonal=True)
    u, w = sol[..., :Dv], sol[..., Dv:]
    qk = jnp.where(tril, jnp.einsum('bhncd,bhnsd->bhncs', q, k) * decay, 0.0)
    q_dec = q * jnp.exp(G)[..., None]
    k_tail = k * jnp.exp(G[..., -1:] - G)[..., None]
    g_last = jnp.exp(G[..., -1])

    def step(state, xs):
        q_n, k_n, u_n, w_n, qk_n, gl_n = xs
        v_new = u_n - jnp.einsum('bhck,bhkv->bhcv', w_n, state)
        o_n = (jnp.einsum('bhck,bhkv->bhcv', q_n, state)
               + jnp.einsum('bhcs,bhsv->bhcv', qk_n, v_new))
        state = state * gl_n[..., None, None] + jnp.einsum('bhck,bhcv->bhkv', k_n, v_new)
        return state, o_n

    xs = tuple(jnp.moveaxis(t, 2, 0) for t in (q_dec, k_tail, u, w, qk, g_last))
    state0 = jnp.zeros((B, H, Dk, Dv), jnp.float32)
    _, o = lax.scan(step, state0, xs)
    return o.transpose(1, 0, 3, 2, 4).reshape(B, S, H, Dv)


def gated_deltanet(qkv, a, b, z, conv_w, a_log, dt_bias, norm_g):
    B, S = qkv.shape[0], qkv.shape[1]
    dtype = qkv.dtype
    qkv = jax.nn.silu(causal_conv(qkv, conv_w)).astype(jnp.float32)
    q = l2_normalize(qkv[..., :DN_QK_COLS].reshape(B, S, DN_HEADS, DN_DK)) * (DN_DK ** -0.5)
    k = l2_normalize(qkv[..., DN_QK_COLS:2 * DN_QK_COLS].reshape(B, S, DN_HEADS, DN_DK))
    v = qkv[..., 2 * DN_QK_COLS:].reshape(B, S, DN_HEADS, DN_DV)
    beta = jax.nn.sigmoid(b.astype(jnp.float32))
    g = -jnp.exp(a_log.astype(jnp.float32)) * jax.nn.softplus(
        a.astype(jnp.float32) + dt_bias.astype(jnp.float32))
    o = chunk_gated_delta_rule(q, k, v, beta, g)
    zf = z.astype(jnp.float32).reshape(B, S, DN_HEADS, DN_DV)
    o = rms_norm(o, norm_g.astype(jnp.float32)) * jax.nn.silu(zf)
    return o.reshape(B, S, DN_V_COLS).astype(dtype)


def token_mixer(h, w_in, kv_norm_g, w_uk, w_uv, conv_w, a_log, dt_bias, dn_norm_g,
                w_br_att, w_br_dn, w_out):
    B, S = h.shape[0], h.shape[1]
    proj = h @ w_in
    (q_att, c_kv, q_idx, k_idx, w_idx, qkv, a, b, z, gates) = jnp.split(proj, SPLITS, axis=-1)
    c_kv = rms_norm(c_kv, kv_norm_g)
    y_att = dsa_attention(q_att.reshape(B, S, ATT_HEADS, ATT_HEAD_DIM), c_kv,
                          q_idx.reshape(B, S, IDX_HEADS, IDX_DIM), k_idx,
                          w_idx * INDEX_SCALE, w_uk, w_uv)
    y_dn = gated_deltanet(qkv, a, b, z, conv_w, a_log, dt_bias, dn_norm_g)
    gates = jax.nn.sigmoid(gates)
    g_att, g_dn = gates[..., :D_MODEL], gates[..., D_MODEL:]
    merged = g_att * (y_att @ w_br_att) + g_dn * (y_dn @ w_br_dn)
    return merged @ w_out


def hier_moe(h, w_route_grp, w_route_exp, w_gate, w_up, w_down):
    B, S, D = h.shape
    n_tok = B * S
    xt = h.reshape(n_tok, D)
    p_grp = jax.nn.softmax((xt @ w_route_grp).astype(jnp.float32), axis=-1)
    top_gp, g_idx = lax.top_k(p_grp, 1)
    exp_logits = (xt @ w_route_exp).astype(jnp.float32).reshape(n_tok, N_GROUPS, EXPERTS_PER_GROUP)
    in_grp = jnp.take_along_axis(exp_logits, g_idx[:, :, None], axis=1)[:, 0]
    top_pe, e_local = lax.top_k(jax.nn.softmax(in_grp, axis=-1), EXPERT_TOPK)
    gates = top_pe / jnp.sum(top_pe, axis=-1, keepdims=True) * top_gp
    e_idx = g_idx * EXPERTS_PER_GROUP + e_local

    m = n_tok * EXPERT_TOPK
    flat_e = e_idx.reshape(m)
    flat_tok = jnp.repeat(jnp.arange(n_tok), EXPERT_TOPK)
    flat_gate = gates.reshape(m)
    order = jnp.argsort(flat_e)
    se = flat_e[order]
    counts = jnp.bincount(flat_e, length=N_EXPERTS)
    pcounts = (counts + MOE_BLOCK - 1) // MOE_BLOCK * MOE_BLOCK
    starts = jnp.cumsum(counts) - counts
    pends = jnp.cumsum(pcounts)
    pstarts = pends - pcounts
    dest = pstarts[se] + jnp.arange(m) - starts[se]
    nb = -(-m // MOE_BLOCK) + N_EXPERTS
    rows = nb * MOE_BLOCK
    tok_buf = jnp.full((rows,), n_tok, jnp.int32).at[dest].set(flat_tok[order])
    gate_buf = jnp.zeros((rows,), jnp.float32).at[dest].set(flat_gate[order])
    blk_e = jnp.minimum(jnp.searchsorted(pends, jnp.arange(nb) * MOE_BLOCK, side='right'),
                        N_EXPERTS - 1)
    x_pad = jnp.concatenate([xt, jnp.zeros((1, D), xt.dtype)], axis=0)
    xs = x_pad[tok_buf].reshape(nb, MOE_BLOCK, D)

    def expert_block(args):
        xb, e = args
        hb = jax.nn.silu(xb @ w_gate[e]) * (xb @ w_up[e])
        return hb @ w_down[e]

    yb = lax.map(expert_block, (xs, blk_e)).reshape(rows, D)
    yb = yb * gate_buf[:, None].astype(yb.dtype)
    out = jnp.zeros((n_tok + 1, D), yb.dtype).at[tok_buf].add(yb)[:n_tok]
    return out.reshape(B, S, D)


def setup_inputs(seed: int = 0) -> dict:
    key = jax.random.key(seed)
    ks = jax.random.split(key, 24)
    f32 = jnp.float32
    L, D = DEPTH, D_MODEL
    beta = (8.0 * DEPTH) ** -0.25

    def nrm(k, shape, scale):
        return jax.random.normal(k, shape, f32) * scale

    dt = jnp.exp(jax.random.uniform(ks[10], (L, DN_HEADS), f32, math.log(1e-3), math.log(1e-1)))
    return {
        'x': nrm(ks[0], (BATCH, SEQ, D), 1.0),
        'c': nrm(ks[1], (BATCH, D), 1.0),
        'w_ada': nrm(ks[2], (L, D, 6 * D), 0.1 * D ** -0.5),
        'b_ada': nrm(ks[3], (L, 6 * D), 0.01),
        'w_in': nrm(ks[4], (L, D, N_IN), D ** -0.5),
        'kv_norm_g': 1.0 + nrm(ks[5], (L, KV_LATENT), 0.02),
        'w_uk': nrm(ks[6], (L, ATT_HEADS, KV_LATENT, ATT_HEAD_DIM), KV_LATENT ** -0.5),
        'w_uv': nrm(ks[7], (L, ATT_HEADS, KV_LATENT, ATT_HEAD_DIM), KV_LATENT ** -0.5),
        'conv_w': nrm(ks[8], (L, CONV_WIDTH, CONV_CH), CONV_WIDTH ** -0.5),
        'a_log': jnp.log(jax.random.uniform(ks[9], (L, DN_HEADS), f32, 1.0, 16.0)),
        'dt_bias': dt + jnp.log(-jnp.expm1(-dt)),
        'dn_norm_g': 1.0 + nrm(ks[11], (L, DN_DV), 0.02),
        'w_br_att': nrm(ks[12], (L, ATT_Q_COLS, D), ATT_Q_COLS ** -0.5),
        'w_br_dn': nrm(ks[13], (L, DN_V_COLS, D), DN_V_COLS ** -0.5),
        'w_out': nrm(ks[14], (L, D, D), beta * D ** -0.5),
        'w_route_grp': nrm(ks[15], (L, D, N_GROUPS), D ** -0.5),
        'w_route_exp': nrm(ks[16], (L, D, N_EXPERTS), D ** -0.5),
        'w_gate': nrm(ks[17], (L, N_EXPERTS, D, EXPERT_FF), D ** -0.5),
        'w_up': nrm(ks[18], (L, N_EXPERTS, D, EXPERT_FF), D ** -0.5),
        'w_down': nrm(ks[19], (L, N_EXPERTS, EXPERT_FF, D), beta * EXPERT_FF ** -0.5),
        'ln_g': 1.0 + nrm(ks[20], (L, 2, D), 0.02),
        'ln_b': nrm(ks[21], (L, 2, D), 0.02),
    }


def reference(x, c, w_ada, b_ada, w_in, kv_norm_g, w_uk, w_uv, conv_w, a_log, dt_bias,
              dn_norm_g, w_br_att, w_br_dn, w_out, w_route_grp, w_route_exp, w_gate,
              w_up, w_down, ln_g, ln_b):
    alpha = (2.0 * DEPTH) ** 0.25
    cond = jax.nn.silu(c)
    for l in range(DEPTH):
        mod = cond @ w_ada[l] + b_ada[l]
        sh1, sc1, gt1, sh2, sc2, gt2 = [m[:, None, :] for m in jnp.split(mod, 6, axis=-1)]
        h = x * (1.0 + sc1) + sh1
        y = token_mixer(h, w_in[l], kv_norm_g[l], w_uk[l], w_uv[l], conv_w[l], a_log[l],
                        dt_bias[l], dn_norm_g[l], w_br_att[l], w_br_dn[l], w_out[l])
        x = layer_norm(alpha * x + (1.0 + gt1) * y, ln_g[l, 0], ln_b[l, 0])
        h = x * (1.0 + sc2) + sh2
        y = hier_moe(h, w_route_grp[l], w_route_exp[l], w_gate[l], w_up[l], w_down[l])
        x = layer_norm(alpha * x + (1.0 + gt2) * y, ln_g[l, 1], ln_b[l, 1])
    return x
```

```python
import functools

import jax
import jax.numpy as jnp
from jax import lax
from jax.experimental import pallas as pl
from jax.experimental.pallas import tpu as pltpu

f32 = jnp.float32
bf16 = jnp.bfloat16
i32 = jnp.int32

ATT_HEADS = 8
ATT_HEAD_DIM = 128
KV_LATENT = 256
IDX_HEADS = 8
IDX_DIM = 64
TOPK_MAX = 256
DN_HEADS = 8
DN_DK = 128
DN_DV = 128
CONV_WIDTH = 4
CHUNK = 64
N_GROUPS = 4
EXPERTS_PER_GROUP = 8
N_EXPERTS = N_GROUPS * EXPERTS_PER_GROUP
EXPERT_TOPK = 2
EXPERT_FF = 512
MOE_BLOCK = 128
LN_EPS = 1e-5
RMS_EPS = 1e-6
ATT_SCALE = ATT_HEAD_DIM ** -0.5
INDEX_SCALE = (IDX_HEADS ** -0.5) * (IDX_DIM ** -0.5)

LANES = 128
SUBLANES = 8
VMEM_LIMIT = 56 * 1024 * 1024
INT_MIN = -(2 ** 31)
NEG_BIG = -1e30

SM_KIDX = 0
SM_WIDX = IDX_DIM
SM_A = SM_WIDX + IDX_HEADS
SM_B = SM_A + DN_HEADS

_NT = (((1,), (1,)), ((), ()))
_TN = (((0,), (0,)), ((), ()))


def _dot(a, b, dims=None, exact=False):
    if dims is None:
        dims = (((a.ndim - 1,), (0,)), ((), ()))
    if exact:
        return lax.dot_general(a.astype(f32), b.astype(f32), dims,
                               precision=lax.Precision.HIGHEST, preferred_element_type=f32)
    return lax.dot_general(a.astype(bf16), b.astype(bf16), dims, preferred_element_type=f32)


def _sigmoid(x):
    return 1.0 / (1.0 + jnp.exp(-x))


def _silu(x):
    return x * _sigmoid(x)


def _cparams(sem):
    return pltpu.CompilerParams(dimension_semantics=sem, vmem_limit_bytes=VMEM_LIMIT)


def _ada_kernel(c_ref, w_ref, b_ref, o_ref):
    cond = _silu(c_ref[...])
    o_ref[0] = _dot(cond, w_ref[0], exact=True) + b_ref[0]


def _ada(c, w_ada, b_ada):
    depth, d, n = w_ada.shape
    b = c.shape[0]
    rows = max(SUBLANES, -(-b // SUBLANES) * SUBLANES)
    cp = jnp.zeros((rows, d), f32).at[:b].set(c)
    tn = 1536
    out = pl.pallas_call(
        _ada_kernel,
        grid=(depth, n // tn),
        in_specs=[pl.BlockSpec((rows, d), lambda l, j: (0, 0)),
                  pl.BlockSpec((1, d, tn), lambda l, j: (l, 0, j)),
                  pl.BlockSpec((1, 1, tn), lambda l, j: (l, 0, j))],
        out_specs=pl.BlockSpec((1, rows, tn), lambda l, j: (l, 0, j)),
        out_shape=jax.ShapeDtypeStruct((depth, rows, n), f32),
        compiler_params=_cparams(("parallel", "parallel")),
        name="ada_mod",
    )(cp, w_ada, b_ada.reshape(depth, 1, n))
    return out[:, :b]


def _modmm_kernel(x_ref, sc_ref, sh_ref, w_ref, *rest, epilogue):
    h = x_ref[0] * (1.0 + sc_ref[0]) + sh_ref[0]
    acc = _dot(h, w_ref[...])
    epilogue(acc, *rest)


def _ep_plain(acc, o_ref):
    o_ref[0] = acc.astype(o_ref.dtype)


def _ep_sigmoid(acc, o_ref):
    o_ref[0] = _sigmoid(acc).astype(o_ref.dtype)


def _ep_rmsnorm(acc, g_ref, o_ref):
    n = acc * lax.rsqrt(jnp.mean(acc * acc, axis=-1, keepdims=True) + RMS_EPS)
    o_ref[0] = (n * g_ref[...]).astype(o_ref.dtype)


def _ep_qlat(acc, wuk_ref, o_ref):
    heads = wuk_ref.shape[0]
    for j in range(heads):
        qh = acc[:, j * ATT_HEAD_DIM:(j + 1) * ATT_HEAD_DIM]
        ql = _dot(qh, wuk_ref[j]) * ATT_SCALE
        o_ref[0, :, j * KV_LATENT:(j + 1) * KV_LATENT] = ql.astype(o_ref.dtype)


def _modmm(x, sc, sh, w, epilogue, out_cols, out_dtype, *, tn, out_tn=None, extra=(),
           extra_specs=(), tm=512, name="modmm"):
    bsz, seq, d = x.shape
    n = w.shape[1]
    out_tn = tn if out_tn is None else out_tn
    tm = min(tm, seq)
    return pl.pallas_call(
        functools.partial(_modmm_kernel, epilogue=epilogue),
        grid=(bsz, seq // tm, n // tn),
        in_specs=[pl.BlockSpec((1, tm, d), lambda b, i, j: (b, i, 0)),
                  pl.BlockSpec((1, 1, d), lambda b, i, j: (b, 0, 0)),
                  pl.BlockSpec((1, 1, d), lambda b, i, j: (b, 0, 0)),
                  pl.BlockSpec((d, tn), lambda b, i, j: (0, j)),
                  *extra_specs],
        out_specs=pl.BlockSpec((1, tm, out_tn), lambda b, i, j: (b, i, j)),
        out_shape=jax.ShapeDtypeStruct((bsz, seq, out_cols), out_dtype),
        compiler_params=_cparams(("parallel", "parallel", "parallel")),
        name=name,
    )(x, sc, sh, w, *extra)


def _attn_kernel(qlat_ref, qidx_ref, small_ref, kidx_ref, kv_ref, wuv_ref, o_ref,
                 keys_ref, qs_ref, p_ref, acc_ref, m_ref, l_ref, al_ref, *, tq, tk, topk, seq):
    i = pl.program_id(1)
    q0 = i * tq
    nkb = (q0 + tq + tk - 1) // tk
    int_min = jnp.int32(INT_MIN)
    row = q0 + lax.broadcasted_iota(i32, (tq, tk), 0)
    col = lax.broadcasted_iota(i32, (tq, tk), 1)
    lane = lax.broadcasted_iota(i32, (tq, LANES), 1)
    nl = tk // LANES

    qi = qidx_ref[0]
    wts = small_ref[0][:, SM_WIDX:SM_WIDX + IDX_HEADS] * INDEX_SCALE

    def score_body(kb, carry):
        c0 = pl.multiple_of(kb * tk, tk)
        kblk = kidx_ref[0, pl.ds(c0, tk), :]
        acc = jnp.zeros((tq, tk), f32)
        for h in range(IDX_HEADS):
            s = _dot(qi[:, h * IDX_DIM:(h + 1) * IDX_DIM], kblk, _NT)
            acc = acc + jnp.maximum(s, 0.0) * wts[:, h:h + 1]
        bits = pltpu.bitcast(acc, i32)
        key = bits ^ ((bits >> 31) & jnp.int32(0x7FFFFFFF))
        key = jnp.where(c0 + col <= row, key, int_min)
        keys_ref[:, pl.ds(c0, tk)] = key
        return carry

    lax.fori_loop(0, nkb, score_body, 0)

    def count(pred):
        def body(kb, part):
            c0 = pl.multiple_of(kb * tk, tk)
            blk = keys_ref[:, pl.ds(c0, tk)]
            for j in range(nl):
                part = part + pred(blk[:, j * LANES:(j + 1) * LANES], c0 + j * LANES + lane)
            return part
        part = lax.fori_loop(0, nkb, body, jnp.zeros((tq, LANES), f32))
        return jnp.sum(part, axis=1, keepdims=True)

    kf = jnp.float32(topk)

    def radix_body(it, carry):
        thr, cnt_thr = carry
        cand = thr + (jnp.int32(1) << (31 - it))
        cand_b = jnp.broadcast_to(cand, (tq, LANES))
        cnt = count(lambda k, c: jnp.where(k >= cand_b, 1.0, 0.0))
        ok = cnt >= kf
        return jnp.where(ok, cand, thr), jnp.where(ok, cnt, cnt_thr)

    thr0 = jnp.full((tq, 1), INT_MIN, i32)
    cnt0 = jnp.zeros((tq, 1), f32) + (nkb * tk).astype(f32)
    thr, cnt_thr = lax.fori_loop(0, 32, radix_body, (thr0, cnt0))

    need = jnp.logical_and(cnt_thr > kf, thr > int_min)
    any_need = jnp.max(jnp.where(need, 1.0, 0.0)) > 0.0

    @pl.when(any_need)
    def _():
        thr_b = jnp.broadcast_to(thr, (tq, LANES))
        n_gt = count(lambda k, c: jnp.where(k > thr_b, 1.0, 0.0))
        quota = kf - n_gt

        def cut_body(it, cut):
            cand = cut + (jnp.int32(1) << (seq.bit_length() - 1 - it))
            cand_b = jnp.broadcast_to(cand, (tq, LANES))
            cnt = count(lambda k, c: jnp.where(k == thr_b, jnp.where(c < cand_b, 1.0, 0.0), 0.0))
            return jnp.where(cnt <= quota, cand, cut)

        cut = lax.fori_loop(0, seq.bit_length(), cut_body, jnp.zeros((tq, 1), i32))

        def drop_body(kb, carry):
            c0 = pl.multiple_of(kb * tk, tk)
            blk = keys_ref[:, pl.ds(c0, tk)]
            dropped = jnp.where(c0 + col >= cut, int_min, blk)
            keys_ref[:, pl.ds(c0, tk)] = jnp.where(blk == thr, dropped, blk)
            return carry

        lax.fori_loop(0, nkb, drop_body, 0)

    thr_eff = jnp.maximum(thr, int_min + 1)
    for h in range(ATT_HEADS):
        qs_ref[h * tq:(h + 1) * tq, :] = qlat_ref[0, :, h * KV_LATENT:(h + 1) * KV_LATENT]
    m_ref[...] = jnp.full(m_ref.shape, NEG_BIG, f32)
    l_ref[...] = jnp.zeros(l_ref.shape, f32)
    acc_ref[...] = jnp.zeros(acc_ref.shape, f32)

    def att_body(kb, carry):
        c0 = pl.multiple_of(kb * tk, tk)
        kvb = kv_ref[0, pl.ds(c0, tk), :]
        s = _dot(qs_ref[...], kvb, _NT)
        bias = jnp.where(keys_ref[:, pl.ds(c0, tk)] >= thr_eff, 0.0, NEG_BIG)
        for h in range(ATT_HEADS):
            hs = slice(h * tq, (h + 1) * tq)
            sh = s[hs, :] + bias
            m_prev = m_ref[hs, :]
            m_new = jnp.maximum(m_prev, jnp.max(sh, axis=1, keepdims=True))
            p = jnp.exp(sh - m_new)
            alpha = jnp.exp(m_prev - m_new)
            l_ref[hs, :] = alpha * l_ref[hs, :] + jnp.sum(p, axis=1, keepdims=True)
            m_ref[hs, :] = m_new
            al_ref[hs, :] = alpha
            p_ref[hs, :] = p.astype(bf16)
        acc_ref[...] = al_ref[...] * acc_ref[...] + _dot(p_ref[...], kvb)
        return carry

    lax.fori_loop(0, nkb, att_body, 0)

    o_lat = acc_ref[...] * (1.0 / l_ref[...])
    for h in range(ATT_HEADS):
        y = _dot(o_lat[h * tq:(h + 1) * tq, :], wuv_ref[h])
        o_ref[0, :, h * ATT_HEAD_DIM:(h + 1) * ATT_HEAD_DIM] = y.astype(o_ref.dtype)


def _dsa_attention(q_lat, q_idx, small, k_idx, kv, w_uv, *, tq=128, tk=512):
    bsz, seq, _ = q_lat.shape
    topk = min(TOPK_MAX, seq // 4)
    tk = min(tk, seq)
    assert tk >= topk and seq % tk == 0 and seq % tq == 0
    rows = ATT_HEADS * tq
    return pl.pallas_call(
        functools.partial(_attn_kernel, tq=tq, tk=tk, topk=topk, seq=seq),
        grid=(bsz, seq // tq),
        in_specs=[pl.BlockSpec((1, tq, ATT_HEADS * KV_LATENT), lambda b, i: (b, i, 0)),
                  pl.BlockSpec((1, tq, IDX_HEADS * IDX_DIM), lambda b, i: (b, i, 0)),
                  pl.BlockSpec((1, tq, LANES), lambda b, i: (b, i, 0)),
                  pl.BlockSpec((1, seq, IDX_DIM), lambda b, i: (b, 0, 0)),
                  pl.BlockSpec((1, seq, KV_LATENT), lambda b, i: (b, 0, 0)),
                  pl.BlockSpec((ATT_HEADS, KV_LATENT, ATT_HEAD_DIM), lambda b, i: (0, 0, 0))],
        out_specs=pl.BlockSpec((1, tq, ATT_HEADS * ATT_HEAD_DIM), lambda b, i: (b, i, 0)),
        out_shape=jax.ShapeDtypeStruct((bsz, seq, ATT_HEADS * ATT_HEAD_DIM), bf16),
        scratch_shapes=[pltpu.VMEM((tq, seq), i32),
                        pltpu.VMEM((rows, KV_LATENT), bf16),
                        pltpu.VMEM((rows, tk), bf16),
                        pltpu.VMEM((rows, KV_LATENT), f32),
                        pltpu.VMEM((rows, 1), f32),
                        pltpu.VMEM((rows, 1), f32),
                        pltpu.VMEM((rows, 1), f32)],
        compiler_params=_cparams(("parallel", "arbitrary")),
        name="dsa_attention",
    )(q_lat, q_idx, small, k_idx, kv, w_uv)


def _dnprep_kernel(cur_ref, prev_ref, w_ref, q_ref, k_ref, v_ref, xx_ref, *, ts):
    i = pl.program_id(1)
    halo = SUBLANES
    xx_ref[0:halo, :] = jnp.where(i > 0, prev_ref[0], 0.0)
    xx_ref[halo:halo + ts, :] = cur_ref[0]
    for g in range(3 * DN_HEADS):
        cs = slice(g * LANES, (g + 1) * LANES)
        y = jnp.zeros((ts, LANES), f32)
        for j in range(CONV_WIDTH):
            off = halo - (CONV_WIDTH - 1) + j
            y = y + w_ref[j:j + 1, cs] * xx_ref[off:off + ts, cs]
        y = _silu(y)
        if g < 2 * DN_HEADS:
            y = y * lax.rsqrt(jnp.sum(y * y, axis=1, keepdims=True) + RMS_EPS)
        if g < DN_HEADS:
            q_ref[0, g] = y * (DN_DK ** -0.5)
        elif g < 2 * DN_HEADS:
            k_ref[0, g - DN_HEADS] = y
        else:
            v_ref[0, g - 2 * DN_HEADS] = y


def _dn_prep(qkv, conv_w, *, ts=256):
    bsz, seq, ch = qkv.shape
    ts = min(ts, seq)
    hb = ts // SUBLANES
    head_out = jax.ShapeDtypeStruct((bsz, DN_HEADS, seq, DN_DK), f32)
    head_spec = pl.BlockSpec((1, DN_HEADS, ts, DN_DK), lambda b, i: (b, 0, i, 0))
    return pl.pallas_call(
        functools.partial(_dnprep_kernel, ts=ts),
        grid=(bsz, seq // ts),
        in_specs=[pl.BlockSpec((1, ts, ch), lambda b, i: (b, i, 0)),
                  pl.BlockSpec((1, SUBLANES, ch), lambda b, i: (b, jnp.maximum(i * hb - 1, 0), 0)),
                  pl.BlockSpec((CONV_WIDTH, ch), lambda b, i: (0, 0))],
        out_specs=[head_spec, head_spec, head_spec],
        out_shape=[head_out, head_out, head_out],
        scratch_shapes=[pltpu.VMEM((ts + SUBLANES, ch), f32)],
        compiler_params=_cparams(("parallel", "parallel")),
        name="dn_prep",
    )(qkv, qkv, conv_w)


def _softplus(x):
    return jnp.maximum(x, 0.0) + jnp.log(1.0 + jnp.exp(-jnp.abs(x)))


def _dnchunk_kernel(alog_ref, dtb_ref, q_ref, k_ref, v_ref, ac_ref, bc_ref, ar_ref, br_ref,
                    z_ref, ng_ref, o_ref, state_ref, *, rows):
    h = pl.program_id(1)
    t = pl.program_id(2)

    @pl.when(t == 0)
    def _():
        state_ref[...] = jnp.zeros(state_ref.shape, f32)

    neg_a = -jnp.exp(jnp.zeros((1, 1), f32) + alog_ref[h])
    dtb = dtb_ref[h]
    g_col = neg_a * _softplus(ac_ref[0, 0] + dtb)
    g_row = neg_a * _softplus(ar_ref[0, 0] + dtb)
    beta_col = _sigmoid(bc_ref[0, 0])
    ri = lax.broadcasted_iota(i32, (CHUNK, CHUNK), 0)
    ci = lax.broadcasted_iota(i32, (CHUNK, CHUNK), 1)
    tril = ri >= ci
    stril = ri > ci
    eye = jnp.where(ri == ci, 1.0, 0.0)

    for c in range(rows // CHUNK):
        rs = slice(c * CHUNK, (c + 1) * CHUNK)
        q = q_ref[0, 0, rs, :]
        k = k_ref[0, 0, rs, :]
        v = v_ref[0, 0, rs, :]
        beta = beta_col[rs, :]
        gcum_c = jnp.sum(jnp.where(tril, g_row[:, rs], 0.0), axis=1, keepdims=True)
        gcum_r = jnp.sum(jnp.where(ri <= ci, g_col[rs, :], 0.0), axis=0, keepdims=True)
        decay = jnp.exp(jnp.where(tril, gcum_c - gcum_r, -jnp.inf))
        kb = k * beta
        a = jnp.where(stril, _dot(kb, k, _NT) * decay, 0.0)
        inv = eye - a
        pw = a
        for _ in range(CHUNK.bit_length() - 2):
            pw = _dot(pw, pw, exact=True)
            inv = inv + _dot(inv, pw, exact=True)
        e_c = jnp.exp(gcum_c)
        u = _dot(inv, v * beta, exact=True)
        w = _dot(inv, kb * e_c, exact=True)
        qk = jnp.where(tril, _dot(q, k, _NT) * decay, 0.0)
        g_last = gcum_c[CHUNK - 1:CHUNK, :]
        q_dec = q * e_c
        k_tail = k * jnp.exp(g_last - gcum_c)
        state = state_ref[...]
        v_new = u - _dot(w, state)
        o = _dot(q_dec, state) + _dot(qk, v_new)
        state_ref[...] = state * jnp.exp(g_last) + _dot(k_tail, v_new, _TN)
        on = o * lax.rsqrt(jnp.mean(o * o, axis=1, keepdims=True) + RMS_EPS) * ng_ref[...]
        o_ref[0, rs, :] = (on * _silu(z_ref[0, rs, :])).astype(o_ref.dtype)


def _dn_chunk(q, k, v, a_col, b_col, a_row, b_row, z, a_log, dt_bias, norm_g, *, rows=256):
    bsz, heads, seq, dk = q.shape
    rows = min(rows, seq)
    hs = pl.BlockSpec((1, 1, rows, dk), lambda b, h, t, *_: (b, h, t, 0))
    cs = pl.BlockSpec((1, 1, rows, 1), lambda b, h, t, *_: (b, h, t, 0))
    rsp = pl.BlockSpec((1, 1, 1, rows), lambda b, h, t, *_: (b, h, 0, t))
    zs = pl.BlockSpec((1, rows, DN_DV), lambda b, h, t, *_: (b, t, h))
    return pl.pallas_call(
        functools.partial(_dnchunk_kernel, rows=rows),
        grid_spec=pltpu.PrefetchScalarGridSpec(
            num_scalar_prefetch=2,
            grid=(bsz, heads, seq // rows),
            in_specs=[hs, hs, hs, cs, cs, rsp, rsp, zs,
                      pl.BlockSpec((1, DN_DV), lambda b, h, t, *_: (0, 0))],
            out_specs=zs,
            scratch_shapes=[pltpu.VMEM((DN_DK, DN_DV), f32)]),
        out_shape=jax.ShapeDtypeStruct((bsz, seq, heads * DN_DV), bf16),
        compiler_params=_cparams(("parallel", "parallel", "arbitrary")),
        name="dn_chunk",
    )(a_log, dt_bias, q, k, v, a_col, b_col, a_row, b_row, z, norm_g.reshape(1, DN_DV))


def _layer_norm(r, g, b):
    mu = jnp.mean(r, axis=-1, keepdims=True)
    d = r - mu
    var = jnp.mean(d * d, axis=-1, keepdims=True)
    return d * lax.rsqrt(var + LN_EPS) * g + b


def _lane_min_index(hit, lane):
    return jnp.min(jnp.where(hit, lane, float(LANES)), axis=1, keepdims=True)


def _merge_kernel(ya_ref, yd_ref, ga_ref, gd_ref, x_ref, gt_ref, lng_ref, lnb_ref, sc_ref, sh_ref,
                  wa_ref, wd_ref, wo_ref, wr_ref, x1_ref, h2_ref, eidx_ref, gate_ref, *, alpha):
    merged = (ga_ref[0] * _dot(ya_ref[0], wa_ref[...]) + gd_ref[0] * _dot(yd_ref[0], wd_ref[...]))
    y = _dot(merged, wo_ref[...])
    x1 = _layer_norm(alpha * x_ref[0] + (1.0 + gt_ref[0]) * y, lng_ref[...], lnb_ref[...])
    x1_ref[0] = x1
    h2 = x1 * (1.0 + sc_ref[0]) + sh_ref[0]
    h2_ref[0] = h2
    logits = _dot(h2, wr_ref[...], exact=True)
    lane = lax.broadcasted_iota(i32, logits.shape, 1).astype(f32)
    lg = jnp.where(lane < N_GROUPS, logits, -jnp.inf)
    mg = jnp.max(lg, axis=1, keepdims=True)
    top_gp = 1.0 / jnp.sum(jnp.exp(lg - mg), axis=1, keepdims=True)
    g_idx = _lane_min_index(lg == mg, lane)
    lo = N_GROUPS + g_idx * EXPERTS_PER_GROUP
    in_grp = jnp.logical_and(lane >= lo, lane < lo + EXPERTS_PER_GROUP)
    le = jnp.where(in_grp, logits, -jnp.inf)
    m1 = jnp.max(le, axis=1, keepdims=True)
    i1 = _lane_min_index(le == m1, lane)
    le2 = jnp.where(lane == i1, -jnp.inf, le)
    m2 = jnp.max(le2, axis=1, keepdims=True)
    i2 = _lane_min_index(le2 == m2, lane)
    e2 = jnp.exp(m2 - m1)
    gate1 = top_gp / (1.0 + e2)
    gate2 = top_gp * e2 / (1.0 + e2)
    e_lanes = jnp.where(lane == 0.0, i1 - N_GROUPS, jnp.where(lane == 1.0, i2 - N_GROUPS, 0.0))
    eidx_ref[0] = e_lanes.astype(i32)
    gate_ref[0] = jnp.where(lane == 0.0, gate1, jnp.where(lane == 1.0, gate2, 0.0))


def _merge(y_att, y_dn, gates, x, gt1, ln_g, ln_b, sc2, sh2, w_br_att, w_br_dn, w_out, w_route,
           alpha, *, tm=256):
    bsz, seq, d = x.shape
    tm = min(tm, seq)
    row = lambda b, i: (b, i, 0)
    per_b = pl.BlockSpec((1, 1, d), lambda b, i: (b, 0, 0))
    vec = pl.BlockSpec((1, d), lambda b, i: (0, 0))
    wsp = pl.BlockSpec((d, d), lambda b, i: (0, 0))
    out_f = jax.ShapeDtypeStruct((bsz, seq, d), f32)
    return pl.pallas_call(
        functools.partial(_merge_kernel, alpha=alpha),
        grid=(bsz, seq // tm),
        in_specs=[pl.BlockSpec((1, tm, d), row), pl.BlockSpec((1, tm, d), row),
                  pl.BlockSpec((1, tm, d), lambda b, i: (b, i, 0)),
                  pl.BlockSpec((1, tm, d), lambda b, i: (b, i, 1)),
                  pl.BlockSpec((1, tm, d), row), per_b, vec, vec, per_b, per_b,
                  wsp, wsp, wsp, pl.BlockSpec((d, LANES), lambda b, i: (0, 0))],
        out_specs=[pl.BlockSpec((1, tm, d), row), pl.BlockSpec((1, tm, d), row),
                   pl.BlockSpec((1, tm, LANES), row), pl.BlockSpec((1, tm, LANES), row)],
        out_shape=[out_f, out_f, jax.ShapeDtypeStruct((bsz, seq, LANES), i32),
                   jax.ShapeDtypeStruct((bsz, seq, LANES), f32)],
        compiler_params=_cparams(("parallel", "parallel")),
        name="merge_router",
    )(y_att, y_dn, gates, gates, x, gt1, ln_g, ln_b, sc2, sh2, w_br_att, w_br_dn, w_out, w_route)


def _expert_kernel(blk_e_ref, tok_ref, slot_ref, h_hbm, gate_ref, wg_ref, wu_ref, wd_ref, y_hbm,
                   xbuf, ybuf, gsem, ssem):
    i = pl.program_id(0)
    nb = pl.num_programs(0)
    cur = i % 2

    def gather_copy(blk, buf, r):
        tok = tok_ref[blk * MOE_BLOCK + r]
        return pltpu.make_async_copy(h_hbm.at[pl.ds(tok, 1), :], xbuf.at[buf, pl.ds(r, 1), :],
                                     gsem.at[buf])

    def scatter_copy(blk, buf, r):
        slot = slot_ref[blk * MOE_BLOCK + r]
        return pltpu.make_async_copy(ybuf.at[buf, pl.ds(r, 1), :], y_hbm.at[pl.ds(slot, 1), :],
                                     ssem.at[buf])

    def start_gather(blk, buf):
        def body(r, carry):
            gather_copy(blk, buf, r).start()
            return carry
        lax.fori_loop(0, MOE_BLOCK, body, 0)

    def wait_rows(copy_fn, blk, buf):
        def body(r, carry):
            copy_fn(blk, buf, r).wait()
            return carry
        lax.fori_loop(0, MOE_BLOCK, body, 0)

    @pl.when(i == 0)
    def _():
        start_gather(0, 0)

    @pl.when(i + 1 < nb)
    def _():
        start_gather(i + 1, 1 - cur)

    wait_rows(gather_copy, i, cur)
    xb = xbuf[cur]
    hb = _silu(_dot(xb, wg_ref[0])) * _dot(xb, wu_ref[0])
    y = _dot(hb, wd_ref[0]) * gate_ref[...]

    @pl.when(i >= 2)
    def _():
        wait_rows(scatter_copy, i - 2, cur)

    ybuf[cur] = y

    def sbody(r, carry):
        scatter_copy(i, cur, r).start()
        return carry
    lax.fori_loop(0, MOE_BLOCK, sbody, 0)

    @pl.when(i == nb - 1)
    def _():
        @pl.when(nb >= 2)
        def _():
            wait_rows(scatter_copy, i - 1, 1 - cur)
        wait_rows(scatter_copy, i, cur)


def _experts(h2, blk_e, tok_buf, slot_buf, gate_buf, w_gate, w_up, w_down, n_slots):
    n_tok, d = h2.shape
    nb = blk_e.shape[0]
    ff = w_gate.shape[-1]
    return pl.pallas_call(
        _expert_kernel,
        grid_spec=pltpu.PrefetchScalarGridSpec(
            num_scalar_prefetch=3,
            grid=(nb,),
            in_specs=[pl.BlockSpec(memory_space=pl.ANY),
                      pl.BlockSpec((MOE_BLOCK, 1), lambda i, *_: (i, 0)),
                      pl.BlockSpec((1, d, ff), lambda i, be, *_: (be[i], 0, 0)),
                      pl.BlockSpec((1, d, ff), lambda i, be, *_: (be[i], 0, 0)),
                      pl.BlockSpec((1, ff, d), lambda i, be, *_: (be[i], 0, 0))],
            out_specs=pl.BlockSpec(memory_space=pl.ANY),
            scratch_shapes=[pltpu.VMEM((2, MOE_BLOCK, d), f32),
                            pltpu.VMEM((2, MOE_BLOCK, d), f32),
                            pltpu.SemaphoreType.DMA((2,)),
                            pltpu.SemaphoreType.DMA((2,))]),
        out_shape=jax.ShapeDtypeStruct((n_slots, d), f32),
        compiler_params=_cparams(("arbitrary",)),
        name="moe_experts",
    )(blk_e, tok_buf, slot_buf, h2, gate_buf, w_gate, w_up, w_down)


def _route_plan(e_idx, gates, n_tok):
    m = n_tok * EXPERT_TOPK
    flat_e = e_idx.reshape(m)
    flat_gate = gates.reshape(m)
    order = jnp.argsort(flat_e, stable=True).astype(i32)
    se = flat_e[order]
    counts = jnp.bincount(flat_e, length=N_EXPERTS).astype(i32)
    pcounts = (counts + MOE_BLOCK - 1) // MOE_BLOCK * MOE_BLOCK
    starts = jnp.cumsum(counts) - counts
    pends = jnp.cumsum(pcounts)
    pstarts = pends - pcounts
    dest = pstarts[se] + jnp.arange(m, dtype=i32) - starts[se]
    nb = -(-m // MOE_BLOCK) + N_EXPERTS
    rows = nb * MOE_BLOCK
    assign = jnp.full((rows,), -1, i32).at[dest].set(order)
    is_pad = assign < 0
    pad_rank = jnp.cumsum(is_pad.astype(i32)) - 1
    slot_buf = jnp.where(is_pad, m + pad_rank, assign)
    tok_buf = jnp.where(is_pad, 0, assign // EXPERT_TOPK)
    gate_buf = jnp.where(is_pad, 0.0, flat_gate[jnp.maximum(assign, 0)])
    blk_e = jnp.minimum(jnp.searchsorted(pends, jnp.arange(nb, dtype=i32) * MOE_BLOCK, side='right'),
                        N_EXPERTS - 1).astype(i32)
    return blk_e, tok_buf, slot_buf, gate_buf.reshape(rows, 1), rows - m


def _final_kernel(y0_ref, y1_ref, x_ref, gt_ref, lng_ref, lnb_ref, o_ref, *, alpha):
    y = y0_ref[...] + y1_ref[...]
    o_ref[0] = _layer_norm(alpha * x_ref[0] + (1.0 + gt_ref[0]) * y, lng_ref[...], lnb_ref[...])


def _final(y_pairs, x, gt2, ln_g, ln_b, alpha, *, tm=512):
    bsz, seq, d = x.shape
    tm = min(tm, seq)
    nblk = seq // tm
    per_b = pl.BlockSpec((1, 1, d), lambda b, i: (b, 0, 0))
    vec = pl.BlockSpec((1, d), lambda b, i: (0, 0))
    return pl.pallas_call(
        functools.partial(_final_kernel, alpha=alpha),
        grid=(bsz, nblk),
        in_specs=[pl.BlockSpec((tm, d), lambda b, i: (b * nblk + i, 0)),
                  pl.BlockSpec((tm, d), lambda b, i: (b * nblk + i, 1)),
                  pl.BlockSpec((1, tm, d), lambda b, i: (b, i, 0)), per_b, vec, vec],
        out_specs=pl.BlockSpec((1, tm, d), lambda b, i: (b, i, 0)),
        out_shape=jax.ShapeDtypeStruct((bsz, seq, d), f32),
        compiler_params=_cparams(("parallel", "parallel")),
        name="moe_combine_ln",
    )(y_pairs, y_pairs, x, gt2, ln_g, ln_b)


def _pack_w_in(w_in):
    sizes = (ATT_HEADS * ATT_HEAD_DIM, KV_LATENT, IDX_HEADS * IDX_DIM, IDX_DIM, IDX_HEADS,
             3 * DN_HEADS * DN_DK, DN_HEADS, DN_HEADS, DN_HEADS * DN_DV, 2 * w_in.shape[0])
    offs = [0]
    for s in sizes:
        offs.append(offs[-1] + s)
    seg = [w_in[:, offs[j]:offs[j + 1]] for j in range(len(sizes))]
    w_q, w_ckv, w_qidx, w_kidx, w_widx, w_qkv, w_a, w_b, w_z, w_gates = seg
    pad = jnp.zeros((w_in.shape[0], LANES - (IDX_DIM + IDX_HEADS + 2 * DN_HEADS)), w_in.dtype)
    w_small = jnp.concatenate([w_kidx, w_widx, w_a, w_b, pad], axis=1)
    return tuple(w.astype(bf16) for w in (w_q, w_ckv, w_qidx, w_small, w_qkv, w_z, w_gates))


def kernel(x, c, w_ada, b_ada, w_in, kv_norm_g, w_uk, w_uv, conv_w, a_log, dt_bias, dn_norm_g,
           w_br_att, w_br_dn, w_out, w_route_grp, w_route_exp, w_gate, w_up, w_down, ln_g, ln_b):
    depth = w_in.shape[0]
    bsz, seq, d = x.shape
    n_tok = bsz * seq
    alpha = (2.0 * depth) ** 0.25
    mod = _ada(c, w_ada, b_ada)
    for l in range(depth):
        sh1, sc1, gt1, sh2, sc2, gt2 = [mod[l, :, j * d:(j + 1) * d].reshape(bsz, 1, d)
                                        for j in range(6)]
        w_q, w_ckv, w_qidx, w_small, w_qkv, w_z, w_gates = _pack_w_in(w_in[l])
        mm = functools.partial(_modmm, x, sc1, sh1)
        heads_per_tile = 4
        q_lat = mm(w_q, _ep_qlat, ATT_HEADS * KV_LATENT, bf16, tn=heads_per_tile * ATT_HEAD_DIM,
                   out_tn=heads_per_tile * KV_LATENT,
                   extra=(jnp.swapaxes(w_uk[l], 1, 2).astype(bf16),),
                   extra_specs=(pl.BlockSpec((heads_per_tile, ATT_HEAD_DIM, KV_LATENT),
                                             lambda b, i, j: (j, 0, 0)),), name="proj_qlat")
        kv = mm(w_ckv, _ep_rmsnorm, KV_LATENT, bf16, tn=KV_LATENT,
                extra=(kv_norm_g[l].reshape(1, KV_LATENT),),
                extra_specs=(pl.BlockSpec((1, KV_LATENT), lambda b, i, j: (0, 0)),), name="proj_kv")
        q_idx = mm(w_qidx, _ep_plain, IDX_HEADS * IDX_DIM, bf16, tn=IDX_HEADS * IDX_DIM,
                   name="proj_qidx")
        small = mm(w_small, _ep_plain, LANES, f32, tn=LANES, name="proj_small")
        qkv = mm(w_qkv, _ep_plain, 3 * DN_HEADS * DN_DK, f32, tn=1024, name="proj_qkv")
        z = mm(w_z, _ep_plain, DN_HEADS * DN_DV, f32, tn=1024, name="proj_z")
        gates = mm(w_gates, _ep_sigmoid, 2 * d, f32, tn=1024, name="proj_gates")

        k_idx = small[..., SM_KIDX:SM_KIDX + IDX_DIM].astype(bf16)
        y_att = _dsa_attention(q_lat, q_idx, small, k_idx, kv, w_uv[l].astype(bf16))

        dq, dk, dv = _dn_prep(qkv, conv_w[l])
        a_t = jnp.swapaxes(small[..., SM_A:SM_A + DN_HEADS], 1, 2)
        b_t = jnp.swapaxes(small[..., SM_B:SM_B + DN_HEADS], 1, 2)
        y_dn = _dn_chunk(dq, dk, dv, a_t[..., None], b_t[..., None], a_t[:, :, None, :],
                         b_t[:, :, None, :], z, a_log[l], dt_bias[l], dn_norm_g[l])

        w_route = jnp.zeros((d, LANES), f32)
        w_route = w_route.at[:, :N_GROUPS].set(w_route_grp[l])
        w_route = w_route.at[:, N_GROUPS:N_GROUPS + N_EXPERTS].set(w_route_exp[l])
        x1, h2, e_lanes, g_lanes = _merge(
            y_att, y_dn, gates, x, gt1, ln_g[l, 0].reshape(1, d), ln_b[l, 0].reshape(1, d),
            sc2, sh2, w_br_att[l].astype(bf16), w_br_dn[l].astype(bf16), w_out[l].astype(bf16),
            w_route, alpha)

        e_idx = e_lanes.reshape(n_tok, LANES)[:, :EXPERT_TOPK]
        gate = g_lanes.reshape(n_tok, LANES)[:, :EXPERT_TOPK]
        blk_e, tok_buf, slot_buf, gate_buf, n_pad = _route_plan(e_idx, gate, n_tok)
        y_slots = _experts(h2.reshape(n_tok, d), blk_e, tok_buf, slot_buf, gate_buf,
                           w_gate[l].astype(bf16), w_up[l].astype(bf16), w_down[l].astype(bf16),
                           n_tok * EXPERT_TOPK + n_pad)
        y_pairs = y_slots.reshape(-1, EXPERT_TOPK * d)
        x = _final(y_pairs, x1, gt2, ln_g[l, 1].reshape(1, d), ln_b[l, 1].reshape(1, d), alpha)
    return x
```

```python
import functools

import jax
import jax.numpy as jnp
from jax import lax
from jax.experimental import pallas as pl
from jax.experimental.pallas import tpu as pltpu

f32 = jnp.float32
bf16 = jnp.bfloat16
i32 = jnp.int32

ATT_HEADS = 8
ATT_HEAD_DIM = 128
KV_LATENT = 256
IDX_HEADS = 8
IDX_DIM = 64
TOPK_MAX = 256
DN_HEADS = 8
DN_DK = 128
DN_DV = 128
CONV_WIDTH = 4
CHUNK = 64
N_GROUPS = 4
EXPERTS_PER_GROUP = 8
N_EXPERTS = N_GROUPS * EXPERTS_PER_GROUP
EXPERT_TOPK = 2
EXPERT_FF = 512
MOE_BLOCK = 128
LN_EPS = 1e-5
RMS_EPS = 1e-6
ATT_SCALE = ATT_HEAD_DIM ** -0.5
INDEX_SCALE = (IDX_HEADS ** -0.5) * (IDX_DIM ** -0.5)
LOG2_E = 1.4426950408889634

LANES = 128
SUBLANES = 8
VMEM_LIMIT = 56 * 1024 * 1024
INT_MIN = -(2 ** 31)
NEG_BIG = -1e30

SM_KIDX = 0
SM_WIDX = IDX_DIM
SM_A = SM_WIDX + IDX_HEADS
SM_B = SM_A + DN_HEADS

_NT = (((1,), (1,)), ((), ()))
_TN = (((0,), (0,)), ((), ()))


def _dot(a, b, dims=None, exact=False):
    if dims is None:
        dims = (((a.ndim - 1,), (0,)), ((), ()))
    if exact:
        return lax.dot_general(a.astype(f32), b.astype(f32), dims,
                               precision=lax.Precision.HIGHEST, preferred_element_type=f32)
    return lax.dot_general(a.astype(bf16), b.astype(bf16), dims, preferred_element_type=f32)


def _sigmoid(x):
    return 1.0 / (1.0 + jnp.exp(-x))


def _silu(x):
    return x * _sigmoid(x)


def _cparams(sem):
    return pltpu.CompilerParams(dimension_semantics=sem, vmem_limit_bytes=VMEM_LIMIT)


def _ada_kernel(c_ref, w_ref, b_ref, o_ref):
    cond = _silu(c_ref[...])
    o_ref[0] = _dot(cond, w_ref[0], exact=True) + b_ref[0]


def _ada(c, w_ada, b_ada):
    depth, d, n = w_ada.shape
    b = c.shape[0]
    rows = max(SUBLANES, -(-b // SUBLANES) * SUBLANES)
    cp = jnp.zeros((rows, d), f32).at[:b].set(c)
    tn = 1536
    out = pl.pallas_call(
        _ada_kernel,
        grid=(depth, n // tn),
        in_specs=[pl.BlockSpec((rows, d), lambda l, j: (0, 0)),
                  pl.BlockSpec((1, d, tn), lambda l, j: (l, 0, j)),
                  pl.BlockSpec((1, 1, tn), lambda l, j: (l, 0, j))],
        out_specs=pl.BlockSpec((1, rows, tn), lambda l, j: (l, 0, j)),
        out_shape=jax.ShapeDtypeStruct((depth, rows, n), f32),
        compiler_params=_cparams(("parallel", "parallel")),
        name="ada_mod",
    )(cp, w_ada, b_ada.reshape(depth, 1, n))
    return out[:, :b]


def _modmm_kernel(x_ref, sc_ref, sh_ref, w_ref, *rest, epilogue):
    h = x_ref[0] * (1.0 + sc_ref[0]) + sh_ref[0]
    acc = _dot(h, w_ref[...])
    epilogue(acc, *rest)


def _ep_plain(acc, o_ref):
    o_ref[0] = acc.astype(o_ref.dtype)


def _ep_sigmoid(acc, o_ref):
    o_ref[0] = _sigmoid(acc).astype(o_ref.dtype)


def _ep_rmsnorm(acc, g_ref, o_ref):
    n = acc * lax.rsqrt(jnp.mean(acc * acc, axis=-1, keepdims=True) + RMS_EPS)
    o_ref[0] = (n * g_ref[...]).astype(o_ref.dtype)


def _ep_qlat(acc, wuk_ref, o_ref):
    heads = wuk_ref.shape[0]
    for j in range(heads):
        qh = acc[:, j * ATT_HEAD_DIM:(j + 1) * ATT_HEAD_DIM]
        ql = _dot(qh, wuk_ref[j]) * (ATT_SCALE * LOG2_E)
        o_ref[0, :, j * KV_LATENT:(j + 1) * KV_LATENT] = ql.astype(o_ref.dtype)


def _modmm(x, sc, sh, w, epilogue, out_cols, out_dtype, *, tn, out_tn=None, extra=(),
           extra_specs=(), tm=512, name="modmm"):
    bsz, seq, d = x.shape
    n = w.shape[1]
    out_tn = tn if out_tn is None else out_tn
    tm = min(tm, seq)
    return pl.pallas_call(
        functools.partial(_modmm_kernel, epilogue=epilogue),
        grid=(bsz, seq // tm, n // tn),
        in_specs=[pl.BlockSpec((1, tm, d), lambda b, i, j: (b, i, 0)),
                  pl.BlockSpec((1, 1, d), lambda b, i, j: (b, 0, 0)),
                  pl.BlockSpec((1, 1, d), lambda b, i, j: (b, 0, 0)),
                  pl.BlockSpec((d, tn), lambda b, i, j: (0, j)),
                  *extra_specs],
        out_specs=pl.BlockSpec((1, tm, out_tn), lambda b, i, j: (b, i, j)),
        out_shape=jax.ShapeDtypeStruct((bsz, seq, out_cols), out_dtype),
        compiler_params=_cparams(("parallel", "parallel", "parallel")),
        name=name,
    )(x, sc, sh, w, *extra)


def _attn_kernel(qlat_ref, qidx_ref, small_ref, kidx_ref, kv_ref, wuv_ref, o_ref,
                 keys_ref, qs_ref, qis_ref, s_ref, bias_ref, p_ref, acc_ref, m_ref, l_ref, al_ref,
                 *, tq, tk, topk, seq):
    i = pl.program_id(1)
    q0 = i * tq
    nkb = (q0 + tq + tk - 1) // tk
    nkb2 = nkb + (nkb & 1)
    int_min = jnp.int32(INT_MIN)
    row = q0 + lax.broadcasted_iota(i32, (tq, tk), 0)
    col = lax.broadcasted_iota(i32, (tq, tk), 1)
    lane = lax.broadcasted_iota(i32, (tq, LANES), 1)
    nl = tk // LANES

    for h in range(IDX_HEADS):
        qis_ref[h * tq:(h + 1) * tq, :] = qidx_ref[0, :, h * IDX_DIM:(h + 1) * IDX_DIM]
    wts = small_ref[0][:, SM_WIDX:SM_WIDX + IDX_HEADS] * INDEX_SCALE

    def score_body(kb, carry):
        c0 = pl.multiple_of(kb * tk, tk)
        kblk = kidx_ref[0, pl.ds(c0, tk), :]
        s = _dot(qis_ref[...], kblk, _NT)
        acc = jnp.zeros((tq, tk), f32)
        for h in range(IDX_HEADS):
            acc = acc + jnp.maximum(s[h * tq:(h + 1) * tq, :], 0.0) * wts[:, h:h + 1]
        bits = pltpu.bitcast(acc, i32)
        key = bits ^ ((bits >> 31) & jnp.int32(0x7FFFFFFF))
        key = jnp.where(c0 + col <= row, key, int_min)
        keys_ref[:, pl.ds(c0, tk)] = key
        return carry

    lax.fori_loop(0, nkb2, score_body, 0)

    def count(pred):
        def body(kb, part):
            c0 = pl.multiple_of(kb * tk, tk)
            blk = keys_ref[:, pl.ds(c0, tk)]
            for j in range(nl):
                part = part + pred(blk[:, j * LANES:(j + 1) * LANES], c0 + j * LANES + lane)
            return part
        part = lax.fori_loop(0, nkb, body, jnp.zeros((tq, LANES), f32))
        return jnp.sum(part, axis=1, keepdims=True)

    kf = jnp.float32(topk)

    def radix_body(it, carry):
        thr, cnt_thr = carry
        cand = thr + (jnp.int32(1) << (31 - it))
        cand_b = jnp.broadcast_to(cand, (tq, LANES))
        cnt = count(lambda k, c: jnp.where(k >= cand_b, 1.0, 0.0))
        ok = cnt >= kf
        return jnp.where(ok, cand, thr), jnp.where(ok, cnt, cnt_thr)

    thr0 = jnp.full((tq, 1), INT_MIN, i32)
    cnt0 = jnp.zeros((tq, 1), f32) + (nkb * tk).astype(f32)
    thr, cnt_thr = lax.fori_loop(0, 32, radix_body, (thr0, cnt0))

    need = jnp.logical_and(cnt_thr > kf, thr > int_min)
    any_need = jnp.max(jnp.where(need, 1.0, 0.0)) > 0.0

    @pl.when(any_need)
    def _():
        thr_b = jnp.broadcast_to(thr, (tq, LANES))
        n_gt = count(lambda k, c: jnp.where(k > thr_b, 1.0, 0.0))
        quota = kf - n_gt

        def cut_body(it, cut):
            cand = cut + (jnp.int32(1) << (seq.bit_length() - 1 - it))
            cand_b = jnp.broadcast_to(cand, (tq, LANES))
            cnt = count(lambda k, c: jnp.where(k == thr_b, jnp.where(c < cand_b, 1.0, 0.0), 0.0))
            return jnp.where(cnt <= quota, cand, cut)

        cut = lax.fori_loop(0, seq.bit_length(), cut_body, jnp.zeros((tq, 1), i32))

        def drop_body(kb, carry):
            c0 = pl.multiple_of(kb * tk, tk)
            blk = keys_ref[:, pl.ds(c0, tk)]
            dropped = jnp.where(c0 + col >= cut, int_min, blk)
            keys_ref[:, pl.ds(c0, tk)] = jnp.where(blk == thr, dropped, blk)
            return carry

        lax.fori_loop(0, nkb, drop_body, 0)

    thr_eff = jnp.maximum(thr, int_min + 1)
    for h in range(ATT_HEADS):
        qs_ref[h * tq:(h + 1) * tq, :] = qlat_ref[0, :, h * KV_LATENT:(h + 1) * KV_LATENT]
    m_ref[...] = jnp.full(m_ref.shape, NEG_BIG, f32)
    l_ref[...] = jnp.zeros(l_ref.shape, f32)
    acc_ref[...] = jnp.zeros(acc_ref.shape, f32)

    p_ref[1] = jnp.zeros(p_ref.shape[1:], bf16)
    al_ref[1] = jnp.ones(al_ref.shape[1:], f32)

    def kv_block(kb):
        return kv_ref[0, pl.ds(pl.multiple_of(kb * tk, tk), tk), :]

    def accumulate(kb, slot):
        pv = _dot(p_ref[slot], kv_block(kb))
        for j in range(KV_LATENT // LANES):
            js = slice(j * LANES, (j + 1) * LANES)
            acc_ref[:, js] = al_ref[slot] * acc_ref[:, js] + pv[:, js]

    def half_step(kb, cur, nxt):
        s_ref[nxt] = _dot(qs_ref[...], kv_block(jnp.minimum(kb + 1, nkb2 - 1)), _NT)
        accumulate(jnp.maximum(kb - 1, 0), nxt)
        c0 = pl.multiple_of(kb * tk, tk)
        bias_ref[...] = jnp.where(keys_ref[:, pl.ds(c0, tk)] >= thr_eff, 0.0, NEG_BIG)
        for h in range(ATT_HEADS):
            hs = slice(h * tq, (h + 1) * tq)
            mx = None
            for j in range(nl):
                js = slice(j * LANES, (j + 1) * LANES)
                t = s_ref[cur, hs, js] + bias_ref[:, js]
                s_ref[cur, hs, js] = t
                mx = t if mx is None else jnp.maximum(mx, t)
            m_prev = m_ref[hs, :]
            m_new = jnp.maximum(m_prev, jnp.max(mx, axis=1, keepdims=True))
            m_ref[hs, :] = m_new
            al_ref[cur, hs, :] = jnp.exp2(m_prev - m_new)
        for h in range(ATT_HEADS):
            hs = slice(h * tq, (h + 1) * tq)
            m_new = m_ref[hs, :]
            ps = None
            for j in range(nl):
                js = slice(j * LANES, (j + 1) * LANES)
                p = jnp.exp2(s_ref[cur, hs, js] - m_new)
                p_ref[cur, hs, js] = p.astype(bf16)
                ps = p if ps is None else ps + p
            l_ref[hs, :] = al_ref[cur, hs, :] * l_ref[hs, :] + ps

    def att_body(j, carry):
        half_step(2 * j, 0, 1)
        half_step(2 * j + 1, 1, 0)
        return carry

    s_ref[0] = _dot(qs_ref[...], kv_block(0), _NT)
    lax.fori_loop(0, nkb2 // 2, att_body, 0)
    accumulate(nkb2 - 1, 1)

    inv_l = 1.0 / jnp.sum(l_ref[...], axis=1, keepdims=True)
    o_lat = acc_ref[...] * inv_l
    for h in range(ATT_HEADS):
        y = _dot(o_lat[h * tq:(h + 1) * tq, :], wuv_ref[h])
        o_ref[0, :, h * ATT_HEAD_DIM:(h + 1) * ATT_HEAD_DIM] = y.astype(o_ref.dtype)


def _dsa_attention(q_lat, q_idx, small, k_idx, kv, w_uv, *, tq=128, tk=512):
    bsz, seq, _ = q_lat.shape
    topk = min(TOPK_MAX, seq // 4)
    tk = min(tk, seq)
    assert tk >= topk and seq % (2 * tk) == 0 and seq % tq == 0
    rows = ATT_HEADS * tq
    return pl.pallas_call(
        functools.partial(_attn_kernel, tq=tq, tk=tk, topk=topk, seq=seq),
        grid=(bsz, seq // tq),
        in_specs=[pl.BlockSpec((1, tq, ATT_HEADS * KV_LATENT), lambda b, i: (b, i, 0)),
                  pl.BlockSpec((1, tq, IDX_HEADS * IDX_DIM), lambda b, i: (b, i, 0)),
                  pl.BlockSpec((1, tq, LANES), lambda b, i: (b, i, 0)),
                  pl.BlockSpec((1, seq, IDX_DIM), lambda b, i: (b, 0, 0)),
                  pl.BlockSpec((1, seq, KV_LATENT), lambda b, i: (b, 0, 0)),
                  pl.BlockSpec((ATT_HEADS, KV_LATENT, ATT_HEAD_DIM), lambda b, i: (0, 0, 0))],
        out_specs=pl.BlockSpec((1, tq, ATT_HEADS * ATT_HEAD_DIM), lambda b, i: (b, i, 0)),
        out_shape=jax.ShapeDtypeStruct((bsz, seq, ATT_HEADS * ATT_HEAD_DIM), bf16),
        scratch_shapes=[pltpu.VMEM((tq, seq), i32),
                        pltpu.VMEM((rows, KV_LATENT), bf16),
                        pltpu.VMEM((rows, IDX_DIM), bf16),
                        pltpu.VMEM((2, rows, tk), f32),
                        pltpu.VMEM((tq, tk), f32),
                        pltpu.VMEM((2, rows, tk), bf16),
                        pltpu.VMEM((rows, KV_LATENT), f32),
                        pltpu.VMEM((rows, LANES), f32),
                        pltpu.VMEM((rows, LANES), f32),
                        pltpu.VMEM((2, rows, LANES), f32)],
        compiler_params=_cparams(("parallel", "arbitrary")),
        name="dsa_attention",
    )(q_lat, q_idx, small, k_idx, kv, w_uv)


def _dnprep_kernel(cur_ref, prev_ref, w_ref, q_ref, k_ref, v_ref, xx_ref, *, ts):
    i = pl.program_id(1)
    halo = SUBLANES
    xx_ref[0:halo, :] = jnp.where(i > 0, prev_ref[0], 0.0)
    xx_ref[halo:halo + ts, :] = cur_ref[0]
    for g in range(3 * DN_HEADS):
        cs = slice(g * LANES, (g + 1) * LANES)
        y = jnp.zeros((ts, LANES), f32)
        for j in range(CONV_WIDTH):
            off = halo - (CONV_WIDTH - 1) + j
            y = y + w_ref[j:j + 1, cs] * xx_ref[off:off + ts, cs]
        y = _silu(y)
        if g < 2 * DN_HEADS:
            y = y * lax.rsqrt(jnp.sum(y * y, axis=1, keepdims=True) + RMS_EPS)
        if g < DN_HEADS:
            q_ref[0, g] = y * (DN_DK ** -0.5)
        elif g < 2 * DN_HEADS:
            k_ref[0, g - DN_HEADS] = y
        else:
            v_ref[0, g - 2 * DN_HEADS] = y


def _dn_prep(qkv, conv_w, *, ts=256):
    bsz, seq, ch = qkv.shape
    ts = min(ts, seq)
    hb = ts // SUBLANES
    head_out = jax.ShapeDtypeStruct((bsz, DN_HEADS, seq, DN_DK), f32)
    head_spec = pl.BlockSpec((1, DN_HEADS, ts, DN_DK), lambda b, i: (b, 0, i, 0))
    return pl.pallas_call(
        functools.partial(_dnprep_kernel, ts=ts),
        grid=(bsz, seq // ts),
        in_specs=[pl.BlockSpec((1, ts, ch), lambda b, i: (b, i, 0)),
                  pl.BlockSpec((1, SUBLANES, ch), lambda b, i: (b, jnp.maximum(i * hb - 1, 0), 0)),
                  pl.BlockSpec((CONV_WIDTH, ch), lambda b, i: (0, 0))],
        out_specs=[head_spec, head_spec, head_spec],
        out_shape=[head_out, head_out, head_out],
        scratch_shapes=[pltpu.VMEM((ts + SUBLANES, ch), f32)],
        compiler_params=_cparams(("parallel", "parallel")),
        name="dn_prep",
    )(qkv, qkv, conv_w)


def _softplus(x):
    return jnp.maximum(x, 0.0) + jnp.log(1.0 + jnp.exp(-jnp.abs(x)))


def _dnchunk_kernel(alog_ref, dtb_ref, q_ref, k_ref, v_ref, ac_ref, bc_ref, ar_ref,
                    z_ref, ng_ref, o_ref, state_ref, *, rows):
    t = pl.program_id(1)

    @pl.when(t == 0)
    def _():
        state_ref[...] = jnp.zeros(state_ref.shape, f32)

    ri = lax.broadcasted_iota(i32, (CHUNK, CHUNK), 0)
    ci = lax.broadcasted_iota(i32, (CHUNK, CHUNK), 1)
    tril = ri >= ci
    stril = ri > ci
    n_chunks = rows // CHUNK
    heads = range(DN_HEADS)
    pairs = [(h, c) for c in range(n_chunks) for h in heads]
    rows_of = lambda c: slice(c * CHUNK, (c + 1) * CHUNK)

    gcum_c, decay, kb = {}, {}, {}
    for p in pairs:
        h, c = p
        rs = rows_of(c)
        neg_a = -jnp.exp(jnp.zeros((1, 1), f32) + alog_ref[h])
        dtb = dtb_ref[h]
        g_col = neg_a * _softplus(ac_ref[0, h, rs, :] + dtb)
        g_row = neg_a * _softplus(ar_ref[0, h, :, rs] + dtb)
        gcum_c[p] = jnp.sum(jnp.where(tril, g_row, 0.0), axis=1, keepdims=True)
        gcum_r = jnp.sum(jnp.where(ri <= ci, g_col, 0.0), axis=0, keepdims=True)
        decay[p] = jnp.exp(jnp.where(tril, gcum_c[p] - gcum_r, -jnp.inf))
        kb[p] = k_ref[0, h, rs, :] * _sigmoid(bc_ref[0, h, rs, :])
    a = {p: jnp.where(stril, _dot(kb[p], k_ref[0, p[0], rows_of(p[1]), :], _NT) * decay[p], 0.0)
         for p in pairs}
    qk = {p: jnp.where(tril, _dot(q_ref[0, p[0], rows_of(p[1]), :],
                                  k_ref[0, p[0], rows_of(p[1]), :], _NT) * decay[p], 0.0)
          for p in pairs}
    r = {p: -a[p] for p in pairs}
    pw = a
    for _ in range(CHUNK.bit_length() - 2):
        pw = {p: _dot(pw[p], pw[p]) for p in pairs}
        r = {p: r[p] + pw[p] + _dot(r[p], pw[p]) for p in pairs}
    sol = {}
    for p in pairs:
        h, c = p
        rs = rows_of(c)
        beta = _sigmoid(bc_ref[0, h, rs, :])
        rhs = jnp.concatenate([v_ref[0, h, rs, :] * beta, kb[p] * jnp.exp(gcum_c[p])], axis=1)
        sol[p] = rhs + _dot(r[p], rhs)

    for c in range(n_chunks):
        rs = rows_of(c)
        state = {h: state_ref[h] for h in heads}
        v_new = {h: sol[(h, c)][:, :DN_DV] - _dot(sol[(h, c)][:, DN_DV:], state[h]) for h in heads}
        o_state = {h: _dot(q_ref[0, h, rs, :] * jnp.exp(gcum_c[(h, c)]), state[h]) for h in heads}
        o_local = {h: _dot(qk[(h, c)], v_new[h]) for h in heads}
        for h in heads:
            g = gcum_c[(h, c)]
            g_last = g[CHUNK - 1:CHUNK, :]
            k_tail = k_ref[0, h, rs, :] * jnp.exp(g_last - g)
            state_ref[h] = state[h] * jnp.exp(g_last) + _dot(k_tail, v_new[h], _TN)
        for h in heads:
            hc = slice(h * DN_DV, (h + 1) * DN_DV)
            o = o_state[h] + o_local[h]
            on = o * lax.rsqrt(jnp.mean(o * o, axis=1, keepdims=True) + RMS_EPS) * ng_ref[...]
            o_ref[0, rs, hc] = (on * _silu(z_ref[0, rs, hc])).astype(o_ref.dtype)


def _dn_chunk(q, k, v, a_col, b_col, a_row, z, a_log, dt_bias, norm_g, *, rows=128):
    bsz, heads, seq, dk = q.shape
    rows = min(rows, seq)
    hs = pl.BlockSpec((1, heads, rows, dk), lambda b, t, *_: (b, 0, t, 0))
    cs = pl.BlockSpec((1, heads, rows, 1), lambda b, t, *_: (b, 0, t, 0))
    rsp = pl.BlockSpec((1, heads, 1, rows), lambda b, t, *_: (b, 0, 0, t))
    zs = pl.BlockSpec((1, rows, heads * DN_DV), lambda b, t, *_: (b, t, 0))
    return pl.pallas_call(
        functools.partial(_dnchunk_kernel, rows=rows),
        grid_spec=pltpu.PrefetchScalarGridSpec(
            num_scalar_prefetch=2,
            grid=(bsz, seq // rows),
            in_specs=[hs, hs, hs, cs, cs, rsp, zs,
                      pl.BlockSpec((1, DN_DV), lambda b, t, *_: (0, 0))],
            out_specs=zs,
            scratch_shapes=[pltpu.VMEM((heads, DN_DK, DN_DV), f32)]),
        out_shape=jax.ShapeDtypeStruct((bsz, seq, heads * DN_DV), bf16),
        compiler_params=_cparams(("parallel", "arbitrary")),
        name="dn_chunk",
    )(a_log, dt_bias, q, k, v, a_col, b_col, a_row, z, norm_g.reshape(1, DN_DV))


def _layer_norm(r, g, b):
    mu = jnp.mean(r, axis=-1, keepdims=True)
    d = r - mu
    var = jnp.mean(d * d, axis=-1, keepdims=True)
    return d * lax.rsqrt(var + LN_EPS) * g + b


def _lane_min_index(hit, lane):
    return jnp.min(jnp.where(hit, lane, float(LANES)), axis=1, keepdims=True)


def _merge_kernel(ya_ref, yd_ref, ga_ref, gd_ref, x_ref, gt_ref, lng_ref, lnb_ref, sc_ref, sh_ref,
                  wa_ref, wd_ref, wo_ref, wr_ref, x1_ref, h2_ref, eidx_ref, gate_ref, *, alpha):
    merged = (ga_ref[0] * _dot(ya_ref[0], wa_ref[...]) + gd_ref[0] * _dot(yd_ref[0], wd_ref[...]))
    y = _dot(merged, wo_ref[...])
    x1 = _layer_norm(alpha * x_ref[0] + (1.0 + gt_ref[0]) * y, lng_ref[...], lnb_ref[...])
    x1_ref[0] = x1
    h2 = x1 * (1.0 + sc_ref[0]) + sh_ref[0]
    h2_ref[0] = h2
    logits = _dot(h2, wr_ref[...], exact=True)
    lane = lax.broadcasted_iota(i32, logits.shape, 1).astype(f32)
    lg = jnp.where(lane < N_GROUPS, logits, -jnp.inf)
    mg = jnp.max(lg, axis=1, keepdims=True)
    top_gp = 1.0 / jnp.sum(jnp.exp(lg - mg), axis=1, keepdims=True)
    g_idx = _lane_min_index(lg == mg, lane)
    lo = N_GROUPS + g_idx * EXPERTS_PER_GROUP
    in_grp = jnp.logical_and(lane >= lo, lane < lo + EXPERTS_PER_GROUP)
    le = jnp.where(in_grp, logits, -jnp.inf)
    m1 = jnp.max(le, axis=1, keepdims=True)
    i1 = _lane_min_index(le == m1, lane)
    le2 = jnp.where(lane == i1, -jnp.inf, le)
    m2 = jnp.max(le2, axis=1, keepdims=True)
    i2 = _lane_min_index(le2 == m2, lane)
    e2 = jnp.exp(m2 - m1)
    gate1 = top_gp / (1.0 + e2)
    gate2 = top_gp * e2 / (1.0 + e2)
    e_lanes = jnp.where(lane == 0.0, i1 - N_GROUPS, jnp.where(lane == 1.0, i2 - N_GROUPS, 0.0))
    eidx_ref[0] = e_lanes.astype(i32)
    gate_ref[0] = jnp.where(lane == 0.0, gate1, jnp.where(lane == 1.0, gate2, 0.0))


def _merge(y_att, y_dn, gates, x, gt1, ln_g, ln_b, sc2, sh2, w_br_att, w_br_dn, w_out, w_route,
           alpha, *, tm=256):
    bsz, seq, d = x.shape
    tm = min(tm, seq)
    row = lambda b, i: (b, i, 0)
    per_b = pl.BlockSpec((1, 1, d), lambda b, i: (b, 0, 0))
    vec = pl.BlockSpec((1, d), lambda b, i: (0, 0))
    wsp = pl.BlockSpec((d, d), lambda b, i: (0, 0))
    out_f = jax.ShapeDtypeStruct((bsz, seq, d), f32)
    return pl.pallas_call(
        functools.partial(_merge_kernel, alpha=alpha),
        grid=(bsz, seq // tm),
        in_specs=[pl.BlockSpec((1, tm, d), row), pl.BlockSpec((1, tm, d), row),
                  pl.BlockSpec((1, tm, d), lambda b, i: (b, i, 0)),
                  pl.BlockSpec((1, tm, d), lambda b, i: (b, i, 1)),
                  pl.BlockSpec((1, tm, d), row), per_b, vec, vec, per_b, per_b,
                  wsp, wsp, wsp, pl.BlockSpec((d, LANES), lambda b, i: (0, 0))],
        out_specs=[pl.BlockSpec((1, tm, d), row), pl.BlockSpec((1, tm, d), row),
                   pl.BlockSpec((1, tm, LANES), row), pl.BlockSpec((1, tm, LANES), row)],
        out_shape=[out_f, out_f, jax.ShapeDtypeStruct((bsz, seq, LANES), i32),
                   jax.ShapeDtypeStruct((bsz, seq, LANES), f32)],
        compiler_params=_cparams(("parallel", "parallel")),
        name="merge_router",
    )(y_att, y_dn, gates, gates, x, gt1, ln_g, ln_b, sc2, sh2, w_br_att, w_br_dn, w_out, w_route)


def _expert_kernel(blk_e_ref, tok_ref, slot_ref, h_hbm, gate_ref, wg_ref, wu_ref, wd_ref, y_hbm,
                   xbuf, ybuf, gsem, ssem):
    i = pl.program_id(0)
    nb = pl.num_programs(0)
    cur = i % 2

    def gather_copy(blk, buf, r):
        tok = tok_ref[blk * MOE_BLOCK + r]
        return pltpu.make_async_copy(h_hbm.at[pl.ds(tok, 1), :], xbuf.at[buf, pl.ds(r, 1), :],
                                     gsem.at[buf])

    def scatter_copy(blk, buf, r):
        slot = slot_ref[blk * MOE_BLOCK + r]
        return pltpu.make_async_copy(ybuf.at[buf, pl.ds(r, 1), :], y_hbm.at[pl.ds(slot, 1), :],
                                     ssem.at[buf])

    def start_gather(blk, buf):
        def body(r, carry):
            gather_copy(blk, buf, r).start()
            return carry
        lax.fori_loop(0, MOE_BLOCK, body, 0)

    def wait_rows(copy_fn, blk, buf):
        def body(r, carry):
            copy_fn(blk, buf, r).wait()
            return carry
        lax.fori_loop(0, MOE_BLOCK, body, 0)

    @pl.when(i == 0)
    def _():
        start_gather(0, 0)

    @pl.when(i + 1 < nb)
    def _():
        start_gather(i + 1, 1 - cur)

    wait_rows(gather_copy, i, cur)
    xb = xbuf[cur]
    hb = _silu(_dot(xb, wg_ref[0])) * _dot(xb, wu_ref[0])
    y = _dot(hb, wd_ref[0]) * gate_ref[...]

    @pl.when(i >= 2)
    def _():
        wait_rows(scatter_copy, i - 2, cur)

    ybuf[cur] = y

    def sbody(r, carry):
        scatter_copy(i, cur, r).start()
        return carry
    lax.fori_loop(0, MOE_BLOCK, sbody, 0)

    @pl.when(i == nb - 1)
    def _():
        @pl.when(nb >= 2)
        def _():
            wait_rows(scatter_copy, i - 1, 1 - cur)
        wait_rows(scatter_copy, i, cur)


def _experts(h2, blk_e, tok_buf, slot_buf, gate_buf, w_gate, w_up, w_down, n_slots):
    n_tok, d = h2.shape
    nb = blk_e.shape[0]
    ff = w_gate.shape[-1]
    return pl.pallas_call(
        _expert_kernel,
        grid_spec=pltpu.PrefetchScalarGridSpec(
            num_scalar_prefetch=3,
            grid=(nb,),
            in_specs=[pl.BlockSpec(memory_space=pl.ANY),
                      pl.BlockSpec((MOE_BLOCK, 1), lambda i, *_: (i, 0)),
                      pl.BlockSpec((1, d, ff), lambda i, be, *_: (be[i], 0, 0)),
                      pl.BlockSpec((1, d, ff), lambda i, be, *_: (be[i], 0, 0)),
                      pl.BlockSpec((1, ff, d), lambda i, be, *_: (be[i], 0, 0))],
            out_specs=pl.BlockSpec(memory_space=pl.ANY),
            scratch_shapes=[pltpu.VMEM((2, MOE_BLOCK, d), f32),
                            pltpu.VMEM((2, MOE_BLOCK, d), f32),
                            pltpu.SemaphoreType.DMA((2,)),
                            pltpu.SemaphoreType.DMA((2,))]),
        out_shape=jax.ShapeDtypeStruct((n_slots, d), f32),
        compiler_params=_cparams(("arbitrary",)),
        name="moe_experts",
    )(blk_e, tok_buf, slot_buf, h2, gate_buf, w_gate, w_up, w_down)


def _route_plan(e_idx, gates, n_tok):
    m = n_tok * EXPERT_TOPK
    flat_e = e_idx.reshape(m)
    flat_gate = gates.reshape(m)
    order = jnp.argsort(flat_e, stable=True).astype(i32)
    se = flat_e[order]
    counts = jnp.bincount(flat_e, length=N_EXPERTS).astype(i32)
    pcounts = (counts + MOE_BLOCK - 1) // MOE_BLOCK * MOE_BLOCK
    starts = jnp.cumsum(counts) - counts
    pends = jnp.cumsum(pcounts)
    pstarts = pends - pcounts
    dest = pstarts[se] + jnp.arange(m, dtype=i32) - starts[se]
    nb = -(-m // MOE_BLOCK) + N_EXPERTS
    rows = nb * MOE_BLOCK
    assign = jnp.full((rows,), -1, i32).at[dest].set(order)
    is_pad = assign < 0
    pad_rank = jnp.cumsum(is_pad.astype(i32)) - 1
    slot_buf = jnp.where(is_pad, m + pad_rank, assign)
    tok_buf = jnp.where(is_pad, 0, assign // EXPERT_TOPK)
    gate_buf = jnp.where(is_pad, 0.0, flat_gate[jnp.maximum(assign, 0)])
    blk_e = jnp.minimum(jnp.searchsorted(pends, jnp.arange(nb, dtype=i32) * MOE_BLOCK, side='right'),
                        N_EXPERTS - 1).astype(i32)
    return blk_e, tok_buf, slot_buf, gate_buf.reshape(rows, 1), rows - m


def _final_kernel(y0_ref, y1_ref, x_ref, gt_ref, lng_ref, lnb_ref, o_ref, *, alpha):
    y = y0_ref[...] + y1_ref[...]
    o_ref[0] = _layer_norm(alpha * x_ref[0] + (1.0 + gt_ref[0]) * y, lng_ref[...], lnb_ref[...])


def _final(y_pairs, x, gt2, ln_g, ln_b, alpha, *, tm=512):
    bsz, seq, d = x.shape
    tm = min(tm, seq)
    nblk = seq // tm
    per_b = pl.BlockSpec((1, 1, d), lambda b, i: (b, 0, 0))
    vec = pl.BlockSpec((1, d), lambda b, i: (0, 0))
    return pl.pallas_call(
        functools.partial(_final_kernel, alpha=alpha),
        grid=(bsz, nblk),
        in_specs=[pl.BlockSpec((tm, d), lambda b, i: (b * nblk + i, 0)),
                  pl.BlockSpec((tm, d), lambda b, i: (b * nblk + i, 1)),
                  pl.BlockSpec((1, tm, d), lambda b, i: (b, i, 0)), per_b, vec, vec],
        out_specs=pl.BlockSpec((1, tm, d), lambda b, i: (b, i, 0)),
        out_shape=jax.ShapeDtypeStruct((bsz, seq, d), f32),
        compiler_params=_cparams(("parallel", "parallel")),
        name="moe_combine_ln",
    )(y_pairs, y_pairs, x, gt2, ln_g, ln_b)


def _pack_w_in(w_in):
    sizes = (ATT_HEADS * ATT_HEAD_DIM, KV_LATENT, IDX_HEADS * IDX_DIM, IDX_DIM, IDX_HEADS,
             3 * DN_HEADS * DN_DK, DN_HEADS, DN_HEADS, DN_HEADS * DN_DV, 2 * w_in.shape[0])
    offs = [0]
    for s in sizes:
        offs.append(offs[-1] + s)
    seg = [w_in[:, offs[j]:offs[j + 1]] for j in range(len(sizes))]
    w_q, w_ckv, w_qidx, w_kidx, w_widx, w_qkv, w_a, w_b, w_z, w_gates = seg
    pad = jnp.zeros((w_in.shape[0], LANES - (IDX_DIM + IDX_HEADS + 2 * DN_HEADS)), w_in.dtype)
    w_small = jnp.concatenate([w_kidx, w_widx, w_a, w_b, pad], axis=1)
    return tuple(w.astype(bf16) for w in (w_q, w_ckv, w_qidx, w_small, w_qkv, w_z, w_gates))


def kernel(x, c, w_ada, b_ada, w_in, kv_norm_g, w_uk, w_uv, conv_w, a_log, dt_bias, dn_norm_g,
           w_br_att, w_br_dn, w_out, w_route_grp, w_route_exp, w_gate, w_up, w_down, ln_g, ln_b):
    depth = w_in.shape[0]
    bsz, seq, d = x.shape
    n_tok = bsz * seq
    alpha = (2.0 * depth) ** 0.25
    mod = _ada(c, w_ada, b_ada)
    for l in range(depth):
        sh1, sc1, gt1, sh2, sc2, gt2 = [mod[l, :, j * d:(j + 1) * d].reshape(bsz, 1, d)
                                        for j in range(6)]
        w_q, w_ckv, w_qidx, w_small, w_qkv, w_z, w_gates = _pack_w_in(w_in[l])
        mm = functools.partial(_modmm, x, sc1, sh1)
        heads_per_tile = 4
        q_lat = mm(w_q, _ep_qlat, ATT_HEADS * KV_LATENT, bf16, tn=heads_per_tile * ATT_HEAD_DIM,
                   out_tn=heads_per_tile * KV_LATENT,
                   extra=(jnp.swapaxes(w_uk[l], 1, 2).astype(bf16),),
                   extra_specs=(pl.BlockSpec((heads_per_tile, ATT_HEAD_DIM, KV_LATENT),
                                             lambda b, i, j: (j, 0, 0)),), name="proj_qlat")
        kv = mm(w_ckv, _ep_rmsnorm, KV_LATENT, bf16, tn=KV_LATENT,
                extra=(kv_norm_g[l].reshape(1, KV_LATENT),),
                extra_specs=(pl.BlockSpec((1, KV_LATENT), lambda b, i, j: (0, 0)),), name="proj_kv")
        q_idx = mm(w_qidx, _ep_plain, IDX_HEADS * IDX_DIM, bf16, tn=IDX_HEADS * IDX_DIM,
                   name="proj_qidx")
        small = mm(w_small, _ep_plain, LANES, f32, tn=LANES, name="proj_small")
        qkv = mm(w_qkv, _ep_plain, 3 * DN_HEADS * DN_DK, f32, tn=1024, name="proj_qkv")
        z = mm(w_z, _ep_plain, DN_HEADS * DN_DV, f32, tn=1024, name="proj_z")
        gates = mm(w_gates, _ep_sigmoid, 2 * d, f32, tn=1024, name="proj_gates")

        k_idx = small[..., SM_KIDX:SM_KIDX + IDX_DIM].astype(bf16)
        y_att = _dsa_attention(q_lat, q_idx, small, k_idx, kv, w_uv[l].astype(bf16))

        dq, dk, dv = _dn_prep(qkv, conv_w[l])
        a_t = jnp.swapaxes(small[..., SM_A:SM_A + DN_HEADS], 1, 2)
        b_t = jnp.swapaxes(small[..., SM_B:SM_B + DN_HEADS], 1, 2)
        y_dn = _dn_chunk(dq, dk, dv, a_t[..., None], b_t[..., None], a_t[:, :, None, :],
                         z, a_log[l], dt_bias[l], dn_norm_g[l])

        w_route = jnp.zeros((d, LANES), f32)
        w_route = w_route.at[:, :N_GROUPS].set(w_route_grp[l])
        w_route = w_route.at[:, N_GROUPS:N_GROUPS + N_EXPERTS].set(w_route_exp[l])
        x1, h2, e_lanes, g_lanes = _merge(
            y_att, y_dn, gates, x, gt1, ln_g[l, 0].reshape(1, d), ln_b[l, 0].reshape(1, d),
            sc2, sh2, w_br_att[l].astype(bf16), w_br_dn[l].astype(bf16), w_out[l].astype(bf16),
            w_route, alpha)

        e_idx = e_lanes.reshape(n_tok, LANES)[:, :EXPERT_TOPK]
        gate = g_lanes.reshape(n_tok, LANES)[:, :EXPERT_TOPK]
        blk_e, tok_buf, slot_buf, gate_buf, n_pad = _route_plan(e_idx, gate, n_tok)
        y_slots = _experts(h2.reshape(n_tok, d), blk_e, tok_buf, slot_buf, gate_buf,
                           w_gate[l].astype(bf16), w_up[l].astype(bf16), w_down[l].astype(bf16),
                           n_tok * EXPERT_TOPK + n_pad)
        y_pairs = y_slots.reshape(-1, EXPERT_TOPK * d)
        x = _final(y_pairs, x1, gt2, ln_g[l, 1].reshape(1, d), ln_b[l, 1].reshape(1, d), alpha)
    return x
```

```python
import functools

import jax
import jax.numpy as jnp
from jax import lax
from jax.experimental import pallas as pl
from jax.experimental.pallas import tpu as pltpu

f32 = jnp.float32
bf16 = jnp.bfloat16
i32 = jnp.int32

ATT_HEADS = 8
ATT_HEAD_DIM = 128
KV_LATENT = 256
IDX_HEADS = 8
IDX_DIM = 64
TOPK_MAX = 256
DN_HEADS = 8
DN_DK = 128
DN_DV = 128
CONV_WIDTH = 4
CHUNK = 64
N_GROUPS = 4
EXPERTS_PER_GROUP = 8
N_EXPERTS = N_GROUPS * EXPERTS_PER_GROUP
EXPERT_TOPK = 2
EXPERT_FF = 512
MOE_BLOCK = 128
LN_EPS = 1e-5
RMS_EPS = 1e-6
ATT_SCALE = ATT_HEAD_DIM ** -0.5
INDEX_SCALE = (IDX_HEADS ** -0.5) * (IDX_DIM ** -0.5)
LOG2_E = 1.4426950408889634

LANES = 128
SUBLANES = 8
VMEM_LIMIT = 56 * 1024 * 1024
INT_MIN = -(2 ** 31)
I16_MIN = -(2 ** 15)
i16 = jnp.int16
NEG_BIG = -1e30

SM_KIDX = 0
SM_WIDX = IDX_DIM
SM_A = SM_WIDX + IDX_HEADS
SM_B = SM_A + DN_HEADS

_NT = (((1,), (1,)), ((), ()))
_TN = (((0,), (0,)), ((), ()))


def _dot(a, b, dims=None, exact=False):
    if dims is None:
        dims = (((a.ndim - 1,), (0,)), ((), ()))
    if exact:
        return lax.dot_general(a.astype(f32), b.astype(f32), dims,
                               precision=lax.Precision.HIGHEST, preferred_element_type=f32)
    return lax.dot_general(a.astype(bf16), b.astype(bf16), dims, preferred_element_type=f32)


def _sigmoid(x):
    return 1.0 / (1.0 + jnp.exp(-x))


def _silu(x):
    return x * _sigmoid(x)


def _cparams(sem):
    return pltpu.CompilerParams(dimension_semantics=sem, vmem_limit_bytes=VMEM_LIMIT)


def _ada_kernel(c_ref, w_ref, b_ref, o_ref):
    cond = _silu(c_ref[...])
    o_ref[0] = _dot(cond, w_ref[0], exact=True) + b_ref[0]


def _ada(c, w_ada, b_ada):
    depth, d, n = w_ada.shape
    b = c.shape[0]
    rows = max(SUBLANES, -(-b // SUBLANES) * SUBLANES)
    cp = jnp.zeros((rows, d), f32).at[:b].set(c)
    tn = 1536
    out = pl.pallas_call(
        _ada_kernel,
        grid=(depth, n // tn),
        in_specs=[pl.BlockSpec((rows, d), lambda l, j: (0, 0)),
                  pl.BlockSpec((1, d, tn), lambda l, j: (l, 0, j)),
                  pl.BlockSpec((1, 1, tn), lambda l, j: (l, 0, j))],
        out_specs=pl.BlockSpec((1, rows, tn), lambda l, j: (l, 0, j)),
        out_shape=jax.ShapeDtypeStruct((depth, rows, n), f32),
        compiler_params=_cparams(("parallel", "parallel")),
        name="ada_mod",
    )(cp, w_ada, b_ada.reshape(depth, 1, n))
    return out[:, :b]


PW_Q = 0
PW_CKV = PW_Q + ATT_HEADS * ATT_HEAD_DIM
PW_QIDX = PW_CKV + KV_LATENT
PW_SMALL = PW_QIDX + IDX_HEADS * IDX_DIM
PW_QKV = PW_SMALL + LANES
PW_Z = PW_QKV + 3 * DN_HEADS * DN_DK
PW_GATES = PW_Z + DN_HEADS * DN_DV
PROJ_TN = 1024


def _proj_kernel(x_ref, sc_ref, sh_ref, w_ref, wuk_ref, g_ref,
                 qlat_ref, kv_ref, qidx_ref, small_ref, qkv_ref, z_ref, gates_ref):
    h = (x_ref[0] * (1.0 + sc_ref[0]) + sh_ref[0]).astype(bf16)

    def cols(start, width):
        return _dot(h, w_ref[:, start:start + width])

    q_att = cols(PW_Q, ATT_HEADS * ATT_HEAD_DIM)
    for j in range(ATT_HEADS):
        ql = _dot(q_att[:, j * ATT_HEAD_DIM:(j + 1) * ATT_HEAD_DIM], wuk_ref[j]) * (ATT_SCALE * LOG2_E)
        qlat_ref[0, :, j * KV_LATENT:(j + 1) * KV_LATENT] = ql.astype(qlat_ref.dtype)
    c_kv = cols(PW_CKV, KV_LATENT)
    c_kv = c_kv * lax.rsqrt(jnp.mean(c_kv * c_kv, axis=-1, keepdims=True) + RMS_EPS)
    kv_ref[0] = (c_kv * g_ref[...]).astype(kv_ref.dtype)
    qidx_ref[0] = cols(PW_QIDX, IDX_HEADS * IDX_DIM).astype(qidx_ref.dtype)
    small_ref[0] = cols(PW_SMALL, LANES)
    for c in range(qkv_ref.shape[-1] // PROJ_TN):
        qkv_ref[0, :, c * PROJ_TN:(c + 1) * PROJ_TN] = cols(PW_QKV + c * PROJ_TN, PROJ_TN)
    z_ref[0] = cols(PW_Z, DN_HEADS * DN_DV)
    for c in range(gates_ref.shape[-1] // PROJ_TN):
        gates_ref[0, :, c * PROJ_TN:(c + 1) * PROJ_TN] = _sigmoid(cols(PW_GATES + c * PROJ_TN, PROJ_TN))


def _proj(x, sc, sh, w_packed, w_ukt, kv_norm_g, *, tm=256):
    bsz, seq, d = x.shape
    tm = min(tm, seq)
    n = w_packed.shape[1]
    widths = (ATT_HEADS * KV_LATENT, KV_LATENT, IDX_HEADS * IDX_DIM, LANES, 3 * DN_HEADS * DN_DK,
              DN_HEADS * DN_DV, 2 * d)
    dtypes = (bf16, bf16, bf16, f32, f32, f32, f32)
    row = lambda b, i: (b, i, 0)
    per_b = pl.BlockSpec((1, 1, d), lambda b, i: (b, 0, 0))
    once = pl.Buffered(1)
    return pl.pallas_call(
        _proj_kernel,
        grid=(bsz, seq // tm),
        in_specs=[pl.BlockSpec((1, tm, d), row), per_b, per_b,
                  pl.BlockSpec((d, n), lambda b, i: (0, 0), pipeline_mode=once),
                  pl.BlockSpec(w_ukt.shape, lambda b, i: (0, 0, 0), pipeline_mode=once),
                  pl.BlockSpec((1, KV_LATENT), lambda b, i: (0, 0))],
        out_specs=[pl.BlockSpec((1, tm, wd), row) for wd in widths],
        out_shape=[jax.ShapeDtypeStruct((bsz, seq, wd), dt) for wd, dt in zip(widths, dtypes)],
        compiler_params=_cparams(("parallel", "parallel")),
        name="proj_in",
    )(x, sc, sh, w_packed, w_ukt, kv_norm_g.reshape(1, KV_LATENT))


def _attn_kernel(qlat_ref, qidx_ref, small_ref, kidx_ref, kv_ref, wuv_ref, o_ref,
                 keys_ref, hi_ref, lo_ref, qs_ref, qis_ref, s_ref, bias_ref, p_ref, acc_ref, m_ref,
                 l_ref, al_ref,
                 *, tq, tk, topk, seq):
    i = pl.program_id(1)
    q0 = i * tq
    nkb = (q0 + tq + tk - 1) // tk
    nkb2 = nkb + (nkb & 1)
    int_min = jnp.int32(INT_MIN)
    row = q0 + lax.broadcasted_iota(i32, (tq, tk), 0)
    col = lax.broadcasted_iota(i32, (tq, tk), 1)
    lane = lax.broadcasted_iota(i32, (tq, LANES), 1)
    nl = tk // LANES

    for h in range(IDX_HEADS):
        qis_ref[h * tq:(h + 1) * tq, :] = qidx_ref[0, :, h * IDX_DIM:(h + 1) * IDX_DIM]
    wts = small_ref[0][:, SM_WIDX:SM_WIDX + IDX_HEADS] * INDEX_SCALE

    def score_body(kb, carry):
        c0 = pl.multiple_of(kb * tk, tk)
        kblk = kidx_ref[0, pl.ds(c0, tk), :]
        s = _dot(qis_ref[...], kblk, _NT)
        acc = jnp.zeros((tq, tk), f32)
        for h in range(IDX_HEADS):
            acc = acc + jnp.maximum(s[h * tq:(h + 1) * tq, :], 0.0) * wts[:, h:h + 1]
        bits = pltpu.bitcast(acc, i32)
        key = bits ^ ((bits >> 31) & jnp.int32(0x7FFFFFFF))
        key = jnp.where(c0 + col <= row, key, int_min)
        keys_ref[:, pl.ds(c0, tk)] = key
        hi_ref[:, pl.ds(c0, tk)] = (key >> 16).astype(i16)
        lo_ref[:, pl.ds(c0, tk)] = ((key & jnp.int32(0xFFFF)) + I16_MIN).astype(i16)
        return carry

    lax.fori_loop(0, nkb2, score_body, 0)

    def count(pred):
        def body(kb, part):
            c0 = pl.multiple_of(kb * tk, tk)
            blk = keys_ref[:, pl.ds(c0, tk)]
            for j in range(nl):
                part = part + pred(blk[:, j * LANES:(j + 1) * LANES], c0 + j * LANES + lane)
            return part
        part = lax.fori_loop(0, nkb, body, jnp.zeros((tq, LANES), f32))
        return jnp.sum(part, axis=1, keepdims=True)

    def count16(ref, pred):
        one, zero = jnp.int16(1), jnp.int16(0)

        def body(kb, part):
            blk = ref[:, pl.ds(pl.multiple_of(kb * tk, tk), tk)]
            for j in range(nl):
                part = part + jnp.where(pred(blk[:, j * LANES:(j + 1) * LANES]), one, zero)
            return part
        part = lax.fori_loop(0, nkb, body, jnp.zeros((tq, LANES), i16))
        return jnp.sum(part.astype(f32), axis=1, keepdims=True)

    def lanes16(v):
        return jnp.broadcast_to(v.astype(i16), (tq, LANES))

    def kth_largest16(ref, kth, cnt_floor):
        def body(it, carry):
            thr, cnt_thr = carry
            cand = thr + (jnp.int32(1) << (15 - it))
            cand_b = lanes16(cand)
            cnt = count16(ref, lambda k: k >= cand_b)
            ok = cnt >= kth
            return jnp.where(ok, cand, thr), jnp.where(ok, cnt, cnt_thr)
        return lax.fori_loop(0, 16, body, (jnp.full((tq, 1), I16_MIN, i32), cnt_floor))

    kf = jnp.float32(topk)
    n_all = jnp.zeros((tq, 1), f32) + (nkb * tk).astype(f32)
    thr_hi, cnt_hi = kth_largest16(hi_ref, kf, n_all)
    thr_hi_b = lanes16(thr_hi)
    n_above = count16(hi_ref, lambda k: k > thr_hi_b)

    def low_plane_body(kb, carry):
        for j in range(nl):
            ds = pl.ds(pl.multiple_of(kb * tk + j * LANES, LANES), LANES)
            lo_ref[:, ds] = jnp.where(hi_ref[:, ds] == thr_hi_b, lo_ref[:, ds], jnp.int16(I16_MIN))
        return carry

    lax.fori_loop(0, nkb, low_plane_body, 0)
    thr_lo, cnt_lo = kth_largest16(lo_ref, kf - n_above, cnt_hi - n_above)
    thr = thr_hi * 65536 + (thr_lo - I16_MIN)
    cnt_thr = n_above + cnt_lo

    need = jnp.logical_and(cnt_thr > kf, thr > int_min)
    any_need = jnp.max(jnp.where(need, 1.0, 0.0)) > 0.0

    @pl.when(any_need)
    def _():
        thr_b = jnp.broadcast_to(thr, (tq, LANES))
        n_gt = count(lambda k, c: jnp.where(k > thr_b, 1.0, 0.0))
        quota = kf - n_gt

        def cut_body(it, cut):
            cand = cut + (jnp.int32(1) << (seq.bit_length() - 1 - it))
            cand_b = jnp.broadcast_to(cand, (tq, LANES))
            cnt = count(lambda k, c: jnp.where(k == thr_b, jnp.where(c < cand_b, 1.0, 0.0), 0.0))
            return jnp.where(cnt <= quota, cand, cut)

        cut = lax.fori_loop(0, seq.bit_length(), cut_body, jnp.zeros((tq, 1), i32))

        def drop_body(kb, carry):
            c0 = pl.multiple_of(kb * tk, tk)
            blk = keys_ref[:, pl.ds(c0, tk)]
            dropped = jnp.where(c0 + col >= cut, int_min, blk)
            keys_ref[:, pl.ds(c0, tk)] = jnp.where(blk == thr, dropped, blk)
            return carry

        lax.fori_loop(0, nkb, drop_body, 0)

    thr_eff = jnp.maximum(thr, int_min + 1)
    for h in range(ATT_HEADS):
        qs_ref[h * tq:(h + 1) * tq, :] = qlat_ref[0, :, h * KV_LATENT:(h + 1) * KV_LATENT]
    m_ref[...] = jnp.full(m_ref.shape, NEG_BIG, f32)
    l_ref[...] = jnp.zeros(l_ref.shape, f32)
    acc_ref[...] = jnp.zeros(acc_ref.shape, f32)

    p_ref[1] = jnp.zeros(p_ref.shape[1:], bf16)
    al_ref[1] = jnp.ones(al_ref.shape[1:], f32)

    def kv_block(kb):
        return kv_ref[0, pl.ds(pl.multiple_of(kb * tk, tk), tk), :]

    def accumulate(kb, slot):
        pv = _dot(p_ref[slot], kv_block(kb))
        for j in range(KV_LATENT // LANES):
            js = slice(j * LANES, (j + 1) * LANES)
            acc_ref[:, js] = al_ref[slot] * acc_ref[:, js] + pv[:, js]

    def half_step(kb, cur, nxt):
        s_ref[nxt] = _dot(qs_ref[...], kv_block(jnp.minimum(kb + 1, nkb2 - 1)), _NT)
        accumulate(jnp.maximum(kb - 1, 0), nxt)
        c0 = pl.multiple_of(kb * tk, tk)
        bias_ref[...] = jnp.where(keys_ref[:, pl.ds(c0, tk)] >= thr_eff, 0.0, NEG_BIG)
        for h in range(ATT_HEADS):
            hs = slice(h * tq, (h + 1) * tq)
            mx = None
            for j in range(nl):
                js = slice(j * LANES, (j + 1) * LANES)
                t = s_ref[cur, hs, js] + bias_ref[:, js]
                s_ref[cur, hs, js] = t
                mx = t if mx is None else jnp.maximum(mx, t)
            m_prev = m_ref[hs, :]
            m_new = jnp.maximum(m_prev, jnp.max(mx, axis=1, keepdims=True))
            m_ref[hs, :] = m_new
            al_ref[cur, hs, :] = jnp.exp2(m_prev - m_new)
        for h in range(ATT_HEADS):
            hs = slice(h * tq, (h + 1) * tq)
            m_new = m_ref[hs, :]
            ps = None
            for j in range(nl):
                js = slice(j * LANES, (j + 1) * LANES)
                p = jnp.exp2(s_ref[cur, hs, js] - m_new)
                p_ref[cur, hs, js] = p.astype(bf16)
                ps = p if ps is None else ps + p
            l_ref[hs, :] = al_ref[cur, hs, :] * l_ref[hs, :] + ps

    def att_body(j, carry):
        half_step(2 * j, 0, 1)
        half_step(2 * j + 1, 1, 0)
        return carry

    s_ref[0] = _dot(qs_ref[...], kv_block(0), _NT)
    lax.fori_loop(0, nkb2 // 2, att_body, 0)
    accumulate(nkb2 - 1, 1)

    inv_l = 1.0 / jnp.sum(l_ref[...], axis=1, keepdims=True)
    o_lat = acc_ref[...] * inv_l
    for h in range(ATT_HEADS):
        y = _dot(o_lat[h * tq:(h + 1) * tq, :], wuv_ref[h])
        o_ref[0, :, h * ATT_HEAD_DIM:(h + 1) * ATT_HEAD_DIM] = y.astype(o_ref.dtype)


def _dsa_attention(q_lat, q_idx, small, k_idx, kv, w_uv, *, tq=128, tk=512):
    bsz, seq, _ = q_lat.shape
    topk = min(TOPK_MAX, seq // 4)
    tk = min(tk, seq)
    assert tk >= topk and seq % (2 * tk) == 0 and seq % tq == 0
    rows = ATT_HEADS * tq
    return pl.pallas_call(
        functools.partial(_attn_kernel, tq=tq, tk=tk, topk=topk, seq=seq),
        grid=(bsz, seq // tq),
        in_specs=[pl.BlockSpec((1, tq, ATT_HEADS * KV_LATENT), lambda b, i: (b, i, 0)),
                  pl.BlockSpec((1, tq, IDX_HEADS * IDX_DIM), lambda b, i: (b, i, 0)),
                  pl.BlockSpec((1, tq, LANES), lambda b, i: (b, i, 0)),
                  pl.BlockSpec((1, seq, IDX_DIM), lambda b, i: (b, 0, 0)),
                  pl.BlockSpec((1, seq, KV_LATENT), lambda b, i: (b, 0, 0)),
                  pl.BlockSpec((ATT_HEADS, KV_LATENT, ATT_HEAD_DIM), lambda b, i: (0, 0, 0))],
        out_specs=pl.BlockSpec((1, tq, ATT_HEADS * ATT_HEAD_DIM), lambda b, i: (b, i, 0)),
        out_shape=jax.ShapeDtypeStruct((bsz, seq, ATT_HEADS * ATT_HEAD_DIM), bf16),
        scratch_shapes=[pltpu.VMEM((tq, seq), i32),
                        pltpu.VMEM((tq, seq), i16),
                        pltpu.VMEM((tq, seq), i16),
                        pltpu.VMEM((rows, KV_LATENT), bf16),
                        pltpu.VMEM((rows, IDX_DIM), bf16),
                        pltpu.VMEM((2, rows, tk), f32),
                        pltpu.VMEM((tq, tk), f32),
                        pltpu.VMEM((2, rows, tk), bf16),
                        pltpu.VMEM((rows, KV_LATENT), f32),
                        pltpu.VMEM((rows, LANES), f32),
                        pltpu.VMEM((rows, LANES), f32),
                        pltpu.VMEM((2, rows, LANES), f32)],
        compiler_params=_cparams(("parallel", "arbitrary")),
        name="dsa_attention",
    )(q_lat, q_idx, small, k_idx, kv, w_uv)


def _dnprep_kernel(cur_ref, prev_ref, w_ref, q_ref, k_ref, v_ref, xx_ref, *, ts):
    i = pl.program_id(1)
    halo = SUBLANES
    xx_ref[0:halo, :] = jnp.where(i > 0, prev_ref[0], 0.0)
    xx_ref[halo:halo + ts, :] = cur_ref[0]
    for g in range(3 * DN_HEADS):
        cs = slice(g * LANES, (g + 1) * LANES)
        y = jnp.zeros((ts, LANES), f32)
        for j in range(CONV_WIDTH):
            off = halo - (CONV_WIDTH - 1) + j
            y = y + w_ref[j:j + 1, cs] * xx_ref[off:off + ts, cs]
        y = _silu(y)
        if g < 2 * DN_HEADS:
            y = y * lax.rsqrt(jnp.sum(y * y, axis=1, keepdims=True) + RMS_EPS)
        if g < DN_HEADS:
            q_ref[0, g] = y * (DN_DK ** -0.5)
        elif g < 2 * DN_HEADS:
            k_ref[0, g - DN_HEADS] = y
        else:
            v_ref[0, g - 2 * DN_HEADS] = y


def _dn_prep(qkv, conv_w, *, ts=256):
    bsz, seq, ch = qkv.shape
    ts = min(ts, seq)
    hb = ts // SUBLANES
    head_out = jax.ShapeDtypeStruct((bsz, DN_HEADS, seq, DN_DK), f32)
    head_spec = pl.BlockSpec((1, DN_HEADS, ts, DN_DK), lambda b, i: (b, 0, i, 0))
    return pl.pallas_call(
        functools.partial(_dnprep_kernel, ts=ts),
        grid=(bsz, seq // ts),
        in_specs=[pl.BlockSpec((1, ts, ch), lambda b, i: (b, i, 0)),
                  pl.BlockSpec((1, SUBLANES, ch), lambda b, i: (b, jnp.maximum(i * hb - 1, 0), 0)),
                  pl.BlockSpec((CONV_WIDTH, ch), lambda b, i: (0, 0))],
        out_specs=[head_spec, head_spec, head_spec],
        out_shape=[head_out, head_out, head_out],
        scratch_shapes=[pltpu.VMEM((ts + SUBLANES, ch), f32)],
        compiler_params=_cparams(("parallel", "parallel")),
        name="dn_prep",
    )(qkv, qkv, conv_w)


def _softplus(x):
    return jnp.maximum(x, 0.0) + jnp.log(1.0 + jnp.exp(-jnp.abs(x)))


def _dnchunk_kernel(alog_ref, dtb_ref, q_ref, k_ref, v_ref, small_ref, ar_ref,
                    z_ref, ng_ref, o_ref, state_ref, *, rows):
    t = pl.program_id(1)

    @pl.when(t == 0)
    def _():
        state_ref[...] = jnp.zeros(state_ref.shape, f32)

    ri = lax.broadcasted_iota(i32, (CHUNK, CHUNK), 0)
    ci = lax.broadcasted_iota(i32, (CHUNK, CHUNK), 1)
    tril = ri >= ci
    stril = ri > ci
    n_chunks = rows // CHUNK
    heads = range(DN_HEADS)
    pairs = [(h, c) for c in range(n_chunks) for h in heads]
    rows_of = lambda c: slice(c * CHUNK, (c + 1) * CHUNK)

    gcum_c, decay, kb = {}, {}, {}
    for p in pairs:
        h, c = p
        rs = rows_of(c)
        neg_a = -jnp.exp(jnp.zeros((1, 1), f32) + alog_ref[h])
        dtb = dtb_ref[h]
        g_col = neg_a * _softplus(small_ref[0, rs, SM_A + h:SM_A + h + 1] + dtb)
        g_row = neg_a * _softplus(ar_ref[0, h, :, rs] + dtb)
        gcum_c[p] = jnp.sum(jnp.where(tril, g_row, 0.0), axis=1, keepdims=True)
        gcum_r = jnp.sum(jnp.where(ri <= ci, g_col, 0.0), axis=0, keepdims=True)
        decay[p] = jnp.exp(jnp.where(tril, gcum_c[p] - gcum_r, -jnp.inf))
        kb[p] = k_ref[0, h, rs, :] * _sigmoid(small_ref[0, rs, SM_B + h:SM_B + h + 1])
    a = {p: jnp.where(stril, _dot(kb[p], k_ref[0, p[0], rows_of(p[1]), :], _NT) * decay[p], 0.0)
         for p in pairs}
    qk = {p: jnp.where(tril, _dot(q_ref[0, p[0], rows_of(p[1]), :],
                                  k_ref[0, p[0], rows_of(p[1]), :], _NT) * decay[p], 0.0)
          for p in pairs}
    r = {p: -a[p] for p in pairs}
    pw = a
    for _ in range(CHUNK.bit_length() - 2):
        pw = {p: _dot(pw[p], pw[p]) for p in pairs}
        r = {p: r[p] + pw[p] + _dot(r[p], pw[p]) for p in pairs}
    sol = {}
    for p in pairs:
        h, c = p
        rs = rows_of(c)
        beta = _sigmoid(small_ref[0, rs, SM_B + h:SM_B + h + 1])
        rhs = jnp.concatenate([v_ref[0, h, rs, :] * beta, kb[p] * jnp.exp(gcum_c[p])], axis=1)
        sol[p] = rhs + _dot(r[p], rhs)

    for c in range(n_chunks):
        rs = rows_of(c)
        state = {h: state_ref[h] for h in heads}
        v_new = {h: sol[(h, c)][:, :DN_DV] - _dot(sol[(h, c)][:, DN_DV:], state[h]) for h in heads}
        o_state = {h: _dot(q_ref[0, h, rs, :] * jnp.exp(gcum_c[(h, c)]), state[h]) for h in heads}
        o_local = {h: _dot(qk[(h, c)], v_new[h]) for h in heads}
        for h in heads:
            g = gcum_c[(h, c)]
            g_last = g[CHUNK - 1:CHUNK, :]
            k_tail = k_ref[0, h, rs, :] * jnp.exp(g_last - g)
            state_ref[h] = state[h] * jnp.exp(g_last) + _dot(k_tail, v_new[h], _TN)
        for h in heads:
            hc = slice(h * DN_DV, (h + 1) * DN_DV)
            o = o_state[h] + o_local[h]
            on = o * lax.rsqrt(jnp.mean(o * o, axis=1, keepdims=True) + RMS_EPS) * ng_ref[...]
            o_ref[0, rs, hc] = (on * _silu(z_ref[0, rs, hc])).astype(o_ref.dtype)


def _dn_chunk(q, k, v, small, a_row, z, a_log, dt_bias, norm_g, *, rows=128):
    bsz, heads, seq, dk = q.shape
    rows = min(rows, seq)
    hs = pl.BlockSpec((1, heads, rows, dk), lambda b, t, *_: (b, 0, t, 0))
    sms = pl.BlockSpec((1, rows, LANES), lambda b, t, *_: (b, t, 0))
    rsp = pl.BlockSpec((1, heads, 1, rows), lambda b, t, *_: (b, 0, 0, t))
    zs = pl.BlockSpec((1, rows, heads * DN_DV), lambda b, t, *_: (b, t, 0))
    return pl.pallas_call(
        functools.partial(_dnchunk_kernel, rows=rows),
        grid_spec=pltpu.PrefetchScalarGridSpec(
            num_scalar_prefetch=2,
            grid=(bsz, seq // rows),
            in_specs=[hs, hs, hs, sms, rsp, zs,
                      pl.BlockSpec((1, DN_DV), lambda b, t, *_: (0, 0))],
            out_specs=zs,
            scratch_shapes=[pltpu.VMEM((heads, DN_DK, DN_DV), f32)]),
        out_shape=jax.ShapeDtypeStruct((bsz, seq, heads * DN_DV), bf16),
        compiler_params=_cparams(("parallel", "arbitrary")),
        name="dn_chunk",
    )(a_log, dt_bias, q, k, v, small, a_row, z, norm_g.reshape(1, DN_DV))


def _layer_norm(r, g, b):
    mu = jnp.mean(r, axis=-1, keepdims=True)
    d = r - mu
    var = jnp.mean(d * d, axis=-1, keepdims=True)
    return d * lax.rsqrt(var + LN_EPS) * g + b


def _lane_min_index(hit, lane):
    return jnp.min(jnp.where(hit, lane, float(LANES)), axis=1, keepdims=True)


def _merge_kernel(ya_ref, yd_ref, ga_ref, gd_ref, x_ref, gt_ref, lng_ref, lnb_ref, sc_ref, sh_ref,
                  wa_ref, wd_ref, wo_ref, wr_ref, x1_ref, h2_ref, eidx_ref, gate_ref, *, alpha):
    merged = (ga_ref[0] * _dot(ya_ref[0], wa_ref[...]) + gd_ref[0] * _dot(yd_ref[0], wd_ref[...]))
    y = _dot(merged, wo_ref[...])
    x1 = _layer_norm(alpha * x_ref[0] + (1.0 + gt_ref[0]) * y, lng_ref[...], lnb_ref[...])
    x1_ref[0] = x1
    h2 = x1 * (1.0 + sc_ref[0]) + sh_ref[0]
    h2_ref[0] = h2
    logits = _dot(h2, wr_ref[...], exact=True)
    lane = lax.broadcasted_iota(i32, logits.shape, 1).astype(f32)
    lg = jnp.where(lane < N_GROUPS, logits, -jnp.inf)
    mg = jnp.max(lg, axis=1, keepdims=True)
    top_gp = 1.0 / jnp.sum(jnp.exp(lg - mg), axis=1, keepdims=True)
    g_idx = _lane_min_index(lg == mg, lane)
    lo = N_GROUPS + g_idx * EXPERTS_PER_GROUP
    in_grp = jnp.logical_and(lane >= lo, lane < lo + EXPERTS_PER_GROUP)
    le = jnp.where(in_grp, logits, -jnp.inf)
    m1 = jnp.max(le, axis=1, keepdims=True)
    i1 = _lane_min_index(le == m1, lane)
    le2 = jnp.where(lane == i1, -jnp.inf, le)
    m2 = jnp.max(le2, axis=1, keepdims=True)
    i2 = _lane_min_index(le2 == m2, lane)
    e2 = jnp.exp(m2 - m1)
    gate1 = top_gp / (1.0 + e2)
    gate2 = top_gp * e2 / (1.0 + e2)
    e_lanes = jnp.where(lane == 0.0, i1 - N_GROUPS, jnp.where(lane == 1.0, i2 - N_GROUPS, 0.0))
    eidx_ref[0] = e_lanes.astype(i32)
    gate_ref[0] = jnp.where(lane == 0.0, gate1, jnp.where(lane == 1.0, gate2, 0.0))


def _merge(y_att, y_dn, gates, x, gt1, ln_g, ln_b, sc2, sh2, w_br_att, w_br_dn, w_out, w_route,
           alpha, *, tm=256):
    bsz, seq, d = x.shape
    tm = min(tm, seq)
    row = lambda b, i: (b, i, 0)
    per_b = pl.BlockSpec((1, 1, d), lambda b, i: (b, 0, 0))
    vec = pl.BlockSpec((1, d), lambda b, i: (0, 0))
    wsp = pl.BlockSpec((d, d), lambda b, i: (0, 0))
    out_f = jax.ShapeDtypeStruct((bsz, seq, d), f32)
    return pl.pallas_call(
        functools.partial(_merge_kernel, alpha=alpha),
        grid=(bsz, seq // tm),
        in_specs=[pl.BlockSpec((1, tm, d), row), pl.BlockSpec((1, tm, d), row),
                  pl.BlockSpec((1, tm, d), lambda b, i: (b, i, 0)),
                  pl.BlockSpec((1, tm, d), lambda b, i: (b, i, 1)),
                  pl.BlockSpec((1, tm, d), row), per_b, vec, vec, per_b, per_b,
                  wsp, wsp, wsp, pl.BlockSpec((d, LANES), lambda b, i: (0, 0))],
        out_specs=[pl.BlockSpec((1, tm, d), row), pl.BlockSpec((1, tm, d), row),
                   pl.BlockSpec((1, tm, LANES), row), pl.BlockSpec((1, tm, LANES), row)],
        out_shape=[out_f, out_f, jax.ShapeDtypeStruct((bsz, seq, LANES), i32),
                   jax.ShapeDtypeStruct((bsz, seq, LANES), f32)],
        compiler_params=_cparams(("parallel", "parallel")),
        name="merge_router",
    )(y_att, y_dn, gates, gates, x, gt1, ln_g, ln_b, sc2, sh2, w_br_att, w_br_dn, w_out, w_route)


def _expert_kernel(blk_e_ref, tok_ref, slot_ref, h_hbm, wg_ref, wu_ref, wd_ref, y_hbm,
                   xbuf, ybuf, gsem, ssem):
    i = pl.program_id(0)
    nb = pl.num_programs(0)
    cur = i % 2

    def gather_copy(blk, buf, r):
        tok = tok_ref[blk * MOE_BLOCK + r]
        return pltpu.make_async_copy(h_hbm.at[pl.ds(tok, 1), :], xbuf.at[buf, pl.ds(r, 1), :],
                                     gsem.at[buf])

    def scatter_copy(blk, buf, r):
        slot = slot_ref[blk * MOE_BLOCK + r]
        return pltpu.make_async_copy(ybuf.at[buf, pl.ds(r, 1), :], y_hbm.at[pl.ds(slot, 1), :],
                                     ssem.at[buf])

    def start_rows(copy_fn, blk, buf):
        def body(r, carry):
            copy_fn(blk, buf, r).start()
            return carry
        lax.fori_loop(0, MOE_BLOCK, body, 0, unroll=8)

    def wait_gather(buf):
        pltpu.make_async_copy(h_hbm.at[pl.ds(0, MOE_BLOCK), :], xbuf.at[buf], gsem.at[buf]).wait()

    def wait_scatter(buf):
        pltpu.make_async_copy(ybuf.at[buf], y_hbm.at[pl.ds(0, MOE_BLOCK), :], ssem.at[buf]).wait()

    @pl.when(i == 0)
    def _():
        start_rows(gather_copy, 0, 0)

    @pl.when(i + 1 < nb)
    def _():
        start_rows(gather_copy, i + 1, 1 - cur)

    wait_gather(cur)
    xb = xbuf[cur]
    hb = _silu(_dot(xb, wg_ref[0])) * _dot(xb, wu_ref[0])
    y = _dot(hb, wd_ref[0])

    @pl.when(i >= 2)
    def _():
        wait_scatter(cur)

    ybuf[cur] = y
    start_rows(scatter_copy, i, cur)

    @pl.when(i == nb - 1)
    def _():
        @pl.when(nb >= 2)
        def _():
            wait_scatter(1 - cur)
        wait_scatter(cur)


def _experts(h2, blk_e, tok_buf, slot_buf, w_gate, w_up, w_down, n_slots):
    n_tok, d = h2.shape
    nb = blk_e.shape[0]
    ff = w_gate.shape[-1]
    return pl.pallas_call(
        _expert_kernel,
        grid_spec=pltpu.PrefetchScalarGridSpec(
            num_scalar_prefetch=3,
            grid=(nb,),
            in_specs=[pl.BlockSpec(memory_space=pl.ANY),
                      pl.BlockSpec((1, d, ff), lambda i, be, *_: (be[i], 0, 0)),
                      pl.BlockSpec((1, d, ff), lambda i, be, *_: (be[i], 0, 0)),
                      pl.BlockSpec((1, ff, d), lambda i, be, *_: (be[i], 0, 0))],
            out_specs=pl.BlockSpec(memory_space=pl.ANY),
            scratch_shapes=[pltpu.VMEM((2, MOE_BLOCK, d), f32),
                            pltpu.VMEM((2, MOE_BLOCK, d), f32),
                            pltpu.SemaphoreType.DMA((2,)),
                            pltpu.SemaphoreType.DMA((2,))]),
        out_shape=jax.ShapeDtypeStruct((n_slots, d), f32),
        compiler_params=_cparams(("arbitrary",)),
        name="moe_experts",
    )(blk_e, tok_buf, slot_buf, h2, w_gate, w_up, w_down)


def _route_plan(e_idx, n_tok):
    m = n_tok * EXPERT_TOPK
    flat_e = e_idx.reshape(m)
    order = jnp.argsort(flat_e, stable=True).astype(i32)
    experts = jnp.arange(N_EXPERTS, dtype=i32)
    counts = jnp.sum((flat_e[:, None] == experts[None, :]).astype(i32), axis=0)
    pcounts = (counts + MOE_BLOCK - 1) // MOE_BLOCK * MOE_BLOCK
    starts = jnp.cumsum(counts) - counts
    pends = jnp.cumsum(pcounts)
    pstarts = pends - pcounts
    nb = -(-m // MOE_BLOCK) + N_EXPERTS
    rows = nb * MOE_BLOCK
    blk_raw = jnp.sum((pends[None, :] <= (jnp.arange(nb, dtype=i32) * MOE_BLOCK)[:, None]).astype(i32),
                      axis=1)
    blk_e = jnp.minimum(blk_raw, N_EXPERTS - 1)
    pos = jnp.arange(rows, dtype=i32)
    in_region = jnp.repeat(blk_raw, MOE_BLOCK) < N_EXPERTS
    e_row = jnp.repeat(blk_e, MOE_BLOCK)
    rank = pos - pstarts[e_row]
    valid = jnp.logical_and(in_region, rank < counts[e_row])
    assign = order[jnp.clip(starts[e_row] + rank, 0, m - 1)]
    n_before = jnp.where(in_region, starts[e_row] + jnp.minimum(rank, counts[e_row]), m)
    slot_buf = jnp.where(valid, assign, m + pos - n_before)
    tok_buf = jnp.where(valid, assign // EXPERT_TOPK, 0)
    return blk_e, tok_buf, slot_buf, rows - m


def _final_kernel(y0_ref, y1_ref, g_ref, x_ref, gt_ref, lng_ref, lnb_ref, o_ref, *, alpha):
    y = y0_ref[...] * g_ref[0, :, 0:1] + y1_ref[...] * g_ref[0, :, 1:2]
    o_ref[0] = _layer_norm(alpha * x_ref[0] + (1.0 + gt_ref[0]) * y, lng_ref[...], lnb_ref[...])


def _final(y_pairs, g_lanes, x, gt2, ln_g, ln_b, alpha, *, tm=512):
    bsz, seq, d = x.shape
    tm = min(tm, seq)
    nblk = seq // tm
    per_b = pl.BlockSpec((1, 1, d), lambda b, i: (b, 0, 0))
    vec = pl.BlockSpec((1, d), lambda b, i: (0, 0))
    return pl.pallas_call(
        functools.partial(_final_kernel, alpha=alpha),
        grid=(bsz, nblk),
        in_specs=[pl.BlockSpec((tm, d), lambda b, i: (b * nblk + i, 0)),
                  pl.BlockSpec((tm, d), lambda b, i: (b * nblk + i, 1)),
                  pl.BlockSpec((1, tm, LANES), lambda b, i: (b, i, 0)),
                  pl.BlockSpec((1, tm, d), lambda b, i: (b, i, 0)), per_b, vec, vec],
        out_specs=pl.BlockSpec((1, tm, d), lambda b, i: (b, i, 0)),
        out_shape=jax.ShapeDtypeStruct((bsz, seq, d), f32),
        compiler_params=_cparams(("parallel", "parallel")),
        name="moe_combine_ln",
    )(y_pairs, y_pairs, g_lanes, x, gt2, ln_g, ln_b)


def _pack_w_in(w_in):
    sizes = (ATT_HEADS * ATT_HEAD_DIM, KV_LATENT, IDX_HEADS * IDX_DIM, IDX_DIM, IDX_HEADS,
             3 * DN_HEADS * DN_DK, DN_HEADS, DN_HEADS, DN_HEADS * DN_DV, 2 * w_in.shape[0])
    offs = [0]
    for s in sizes:
        offs.append(offs[-1] + s)
    seg = [w_in[:, offs[j]:offs[j + 1]] for j in range(len(sizes))]
    w_q, w_ckv, w_qidx, w_kidx, w_widx, w_qkv, w_a, w_b, w_z, w_gates = seg
    pad = jnp.zeros((w_in.shape[0], LANES - (IDX_DIM + IDX_HEADS + 2 * DN_HEADS)), w_in.dtype)
    packed = jnp.concatenate([w_q, w_ckv, w_qidx, w_kidx, w_widx, w_a, w_b, pad, w_qkv, w_z, w_gates],
                             axis=1)
    return packed.astype(bf16)


def kernel(x, c, w_ada, b_ada, w_in, kv_norm_g, w_uk, w_uv, conv_w, a_log, dt_bias, dn_norm_g,
           w_br_att, w_br_dn, w_out, w_route_grp, w_route_exp, w_gate, w_up, w_down, ln_g, ln_b):
    depth = w_in.shape[0]
    bsz, seq, d = x.shape
    n_tok = bsz * seq
    alpha = (2.0 * depth) ** 0.25
    mod = _ada(c, w_ada, b_ada)
    for l in range(depth):
        sh1, sc1, gt1, sh2, sc2, gt2 = [mod[l, :, j * d:(j + 1) * d].reshape(bsz, 1, d)
                                        for j in range(6)]
        q_lat, kv, q_idx, small, qkv, z, gates = _proj(
            x, sc1, sh1, _pack_w_in(w_in[l]), jnp.swapaxes(w_uk[l], 1, 2).astype(bf16), kv_norm_g[l])

        k_idx = small[..., SM_KIDX:SM_KIDX + IDX_DIM].astype(bf16)
        y_att = _dsa_attention(q_lat, q_idx, small, k_idx, kv, w_uv[l].astype(bf16))

        dq, dk, dv = _dn_prep(qkv, conv_w[l])
        a_row = jnp.swapaxes(small[..., SM_A:SM_A + DN_HEADS], 1, 2)[:, :, None, :]
        y_dn = _dn_chunk(dq, dk, dv, small, a_row, z, a_log[l], dt_bias[l], dn_norm_g[l])

        w_route = jnp.zeros((d, LANES), f32)
        w_route = w_route.at[:, :N_GROUPS].set(w_route_grp[l])
        w_route = w_route.at[:, N_GROUPS:N_GROUPS + N_EXPERTS].set(w_route_exp[l])
        x1, h2, e_lanes, g_lanes = _merge(
            y_att, y_dn, gates, x, gt1, ln_g[l, 0].reshape(1, d), ln_b[l, 0].reshape(1, d),
            sc2, sh2, w_br_att[l].astype(bf16), w_br_dn[l].astype(bf16), w_out[l].astype(bf16),
            w_route, alpha)

        e_idx = e_lanes.reshape(n_tok, LANES)[:, :EXPERT_TOPK]
        blk_e, tok_buf, slot_buf, n_pad = _route_plan(e_idx, n_tok)
        y_slots = _experts(h2.reshape(n_tok, d), blk_e, tok_buf, slot_buf,
                           w_gate[l].astype(bf16), w_up[l].astype(bf16), w_down[l].astype(bf16),
                           n_tok * EXPERT_TOPK + n_pad)
        y_pairs = y_slots.reshape(-1, EXPERT_TOPK * d)
        x = _final(y_pairs, g_lanes, x1, gt2, ln_g[l, 1].reshape(1, d), ln_b[l, 1].reshape(1, d), alpha)
    return x
```

```python
import functools

import jax
import jax.numpy as jnp
from jax import lax
from jax.experimental import pallas as pl
from jax.experimental.pallas import tpu as pltpu

f32 = jnp.float32
bf16 = jnp.bfloat16
i32 = jnp.int32

ATT_HEADS = 8
ATT_HEAD_DIM = 128
KV_LATENT = 256
IDX_HEADS = 8
IDX_DIM = 64
TOPK_MAX = 256
DN_HEADS = 8
DN_DK = 128
DN_DV = 128
CONV_WIDTH = 4
CHUNK = 64
N_GROUPS = 4
EXPERTS_PER_GROUP = 8
N_EXPERTS = N_GROUPS * EXPERTS_PER_GROUP
EXPERT_TOPK = 2
EXPERT_FF = 512
MOE_BLOCK = 128
LN_EPS = 1e-5
RMS_EPS = 1e-6
ATT_SCALE = ATT_HEAD_DIM ** -0.5
INDEX_SCALE = (IDX_HEADS ** -0.5) * (IDX_DIM ** -0.5)
LOG2_E = 1.4426950408889634

LANES = 128
SUBLANES = 8
VMEM_LIMIT = 56 * 1024 * 1024
INT_MIN = -(2 ** 31)
NEG_BIG = -1e30

SM_KIDX = 0
SM_WIDX = IDX_DIM
SM_A = SM_WIDX + IDX_HEADS
SM_B = SM_A + DN_HEADS

_NT = (((1,), (1,)), ((), ()))
_TN = (((0,), (0,)), ((), ()))


def _dot(a, b, dims=None, exact=False):
    if dims is None:
        dims = (((a.ndim - 1,), (0,)), ((), ()))
    if exact:
        return lax.dot_general(a.astype(f32), b.astype(f32), dims,
                               precision=lax.Precision.HIGHEST, preferred_element_type=f32)
    return lax.dot_general(a.astype(bf16), b.astype(bf16), dims, preferred_element_type=f32)


def _sigmoid(x):
    return 1.0 / (1.0 + jnp.exp(-x))


def _silu(x):
    return x * _sigmoid(x)


def _cparams(sem):
    return pltpu.CompilerParams(dimension_semantics=sem, vmem_limit_bytes=VMEM_LIMIT)


def _ada_kernel(c_ref, w_ref, b_ref, o_ref):
    cond = _silu(c_ref[...])
    o_ref[0] = _dot(cond, w_ref[0], exact=True) + b_ref[0]


def _ada(c, w_ada, b_ada):
    depth, d, n = w_ada.shape
    b = c.shape[0]
    rows = max(SUBLANES, -(-b // SUBLANES) * SUBLANES)
    cp = jnp.zeros((rows, d), f32).at[:b].set(c)
    tn = 1536
    out = pl.pallas_call(
        _ada_kernel,
        grid=(depth, n // tn),
        in_specs=[pl.BlockSpec((rows, d), lambda l, j: (0, 0)),
                  pl.BlockSpec((1, d, tn), lambda l, j: (l, 0, j)),
                  pl.BlockSpec((1, 1, tn), lambda l, j: (l, 0, j))],
        out_specs=pl.BlockSpec((1, rows, tn), lambda l, j: (l, 0, j)),
        out_shape=jax.ShapeDtypeStruct((depth, rows, n), f32),
        compiler_params=_cparams(("parallel", "parallel")),
        name="ada_mod",
    )(cp, w_ada, b_ada.reshape(depth, 1, n))
    return out[:, :b]


PW_Q = 0
PW_CKV = PW_Q + ATT_HEADS * ATT_HEAD_DIM
PW_QIDX = PW_CKV + KV_LATENT
PW_SMALL = PW_QIDX + IDX_HEADS * IDX_DIM
PW_QKV = PW_SMALL + LANES
PW_Z = PW_QKV + 3 * DN_HEADS * DN_DK
PW_GATES = PW_Z + DN_HEADS * DN_DV
PROJ_TN = 1024


def _proj_kernel(x_ref, sc_ref, sh_ref, w_ref, wuk_ref, g_ref,
                 qlat_ref, kv_ref, qidx_ref, small_ref, qkv_ref, z_ref, gates_ref, kidx_ref):
    h = (x_ref[0] * (1.0 + sc_ref[0]) + sh_ref[0]).astype(bf16)

    def cols(start, width):
        return _dot(h, w_ref[:, start:start + width])

    q_att = cols(PW_Q, ATT_HEADS * ATT_HEAD_DIM)
    for j in range(ATT_HEADS):
        ql = _dot(q_att[:, j * ATT_HEAD_DIM:(j + 1) * ATT_HEAD_DIM], wuk_ref[j]) * (ATT_SCALE * LOG2_E)
        qlat_ref[0, :, j * KV_LATENT:(j + 1) * KV_LATENT] = ql.astype(qlat_ref.dtype)
    c_kv = cols(PW_CKV, KV_LATENT)
    c_kv = c_kv * lax.rsqrt(jnp.mean(c_kv * c_kv, axis=-1, keepdims=True) + RMS_EPS)
    kv_ref[0] = (c_kv * g_ref[...]).astype(kv_ref.dtype)
    qidx_ref[0] = cols(PW_QIDX, IDX_HEADS * IDX_DIM).astype(qidx_ref.dtype)
    small = cols(PW_SMALL, LANES)
    small_ref[0] = small
    kidx_ref[0] = small[:, SM_KIDX:SM_KIDX + IDX_DIM].astype(kidx_ref.dtype)
    for c in range(qkv_ref.shape[-1] // PROJ_TN):
        qkv_ref[0, :, c * PROJ_TN:(c + 1) * PROJ_TN] = cols(PW_QKV + c * PROJ_TN, PROJ_TN)
    z_ref[0] = cols(PW_Z, DN_HEADS * DN_DV)
    for c in range(gates_ref.shape[-1] // PROJ_TN):
        gates_ref[0, :, c * PROJ_TN:(c + 1) * PROJ_TN] = _sigmoid(cols(PW_GATES + c * PROJ_TN, PROJ_TN))


def _proj(x, sc, sh, w_packed, w_ukt, kv_norm_g, *, tm=256):
    bsz, seq, d = x.shape
    tm = min(tm, seq)
    n = w_packed.shape[1]
    widths = (ATT_HEADS * KV_LATENT, KV_LATENT, IDX_HEADS * IDX_DIM, LANES, 3 * DN_HEADS * DN_DK,
              DN_HEADS * DN_DV, 2 * d, IDX_DIM)
    dtypes = (bf16, bf16, bf16, f32, f32, f32, f32, bf16)
    row = lambda b, i: (b, i, 0)
    per_b = pl.BlockSpec((1, 1, d), lambda b, i: (b, 0, 0))
    once = pl.Buffered(1)
    return pl.pallas_call(
        _proj_kernel,
        grid=(bsz, seq // tm),
        in_specs=[pl.BlockSpec((1, tm, d), row), per_b, per_b,
                  pl.BlockSpec((d, n), lambda b, i: (0, 0), pipeline_mode=once),
                  pl.BlockSpec(w_ukt.shape, lambda b, i: (0, 0, 0), pipeline_mode=once),
                  pl.BlockSpec((1, KV_LATENT), lambda b, i: (0, 0))],
        out_specs=[pl.BlockSpec((1, tm, wd), row) for wd in widths],
        out_shape=[jax.ShapeDtypeStruct((bsz, seq, wd), dt) for wd, dt in zip(widths, dtypes)],
        compiler_params=_cparams(("parallel", "parallel")),
        name="proj_in",
    )(x, sc, sh, w_packed, w_ukt, kv_norm_g.reshape(1, KV_LATENT))


def _attn_kernel(qlat_ref, qidx_ref, small_ref, kidx_ref, kv_ref, wuv_ref, o_ref,
                 keys_ref, qs_ref, qis_ref, s_ref, bias_ref, p_ref, acc_ref, m_ref, l_ref, al_ref,
                 *, tq, tk, topk, seq):
    i = pl.program_id(1)
    q0 = i * tq
    nkb = (q0 + tq + tk - 1) // tk
    nkb2 = nkb + (nkb & 1)
    int_min = jnp.int32(INT_MIN)
    row = q0 + lax.broadcasted_iota(i32, (tq, tk), 0)
    col = lax.broadcasted_iota(i32, (tq, tk), 1)
    lane = lax.broadcasted_iota(i32, (tq, LANES), 1)
    nl = tk // LANES

    for h in range(IDX_HEADS):
        qis_ref[h * tq:(h + 1) * tq, :] = qidx_ref[0, :, h * IDX_DIM:(h + 1) * IDX_DIM]
    wts = small_ref[0][:, SM_WIDX:SM_WIDX + IDX_HEADS] * INDEX_SCALE

    def score_body(kb, carry):
        c0 = pl.multiple_of(kb * tk, tk)
        kblk = kidx_ref[0, pl.ds(c0, tk), :]
        s = _dot(qis_ref[...], kblk, _NT)
        acc = jnp.zeros((tq, tk), f32)
        for h in range(IDX_HEADS):
            acc = acc + jnp.maximum(s[h * tq:(h + 1) * tq, :], 0.0) * wts[:, h:h + 1]
        bits = pltpu.bitcast(acc, i32)
        key = bits ^ ((bits >> 31) & jnp.int32(0x7FFFFFFF))
        key = jnp.where(c0 + col <= row, key, int_min)
        keys_ref[:, pl.ds(c0, tk)] = key
        return carry

    lax.fori_loop(0, nkb2, score_body, 0)

    def count(pred):
        def body(kb, part):
            c0 = pl.multiple_of(kb * tk, tk)
            blk = keys_ref[:, pl.ds(c0, tk)]
            for j in range(nl):
                part = part + pred(blk[:, j * LANES:(j + 1) * LANES], c0 + j * LANES + lane)
            return part
        part = lax.fori_loop(0, nkb, body, jnp.zeros((tq, LANES), f32))
        return jnp.sum(part, axis=1, keepdims=True)

    kf = jnp.float32(topk)

    def radix_body(it, carry):
        thr, cnt_thr = carry
        cand = thr + (jnp.int32(1) << (31 - it))
        cand_b = jnp.broadcast_to(cand, (tq, LANES))
        cnt = count(lambda k, c: jnp.where(k >= cand_b, 1.0, 0.0))
        ok = cnt >= kf
        return jnp.where(ok, cand, thr), jnp.where(ok, cnt, cnt_thr)

    thr0 = jnp.full((tq, 1), INT_MIN, i32)
    cnt0 = jnp.zeros((tq, 1), f32) + (nkb * tk).astype(f32)
    thr, cnt_thr = lax.fori_loop(0, 32, radix_body, (thr0, cnt0))

    need = jnp.logical_and(cnt_thr > kf, thr > int_min)
    any_need = jnp.max(jnp.where(need, 1.0, 0.0)) > 0.0

    @pl.when(any_need)
    def _():
        thr_b = jnp.broadcast_to(thr, (tq, LANES))
        n_gt = count(lambda k, c: jnp.where(k > thr_b, 1.0, 0.0))
        quota = kf - n_gt

        def cut_body(it, cut):
            cand = cut + (jnp.int32(1) << (seq.bit_length() - 1 - it))
            cand_b = jnp.broadcast_to(cand, (tq, LANES))
            cnt = count(lambda k, c: jnp.where(k == thr_b, jnp.where(c < cand_b, 1.0, 0.0), 0.0))
            return jnp.where(cnt <= quota, cand, cut)

        cut = lax.fori_loop(0, seq.bit_length(), cut_body, jnp.zeros((tq, 1), i32))

        def drop_body(kb, carry):
            c0 = pl.multiple_of(kb * tk, tk)
            blk = keys_ref[:, pl.ds(c0, tk)]
            dropped = jnp.where(c0 + col >= cut, int_min, blk)
            keys_ref[:, pl.ds(c0, tk)] = jnp.where(blk == thr, dropped, blk)
            return carry

        lax.fori_loop(0, nkb, drop_body, 0)

    thr_eff = jnp.maximum(thr, int_min + 1)
    for h in range(ATT_HEADS):
        qs_ref[h * tq:(h + 1) * tq, :] = qlat_ref[0, :, h * KV_LATENT:(h + 1) * KV_LATENT]
    m_ref[...] = jnp.full(m_ref.shape, NEG_BIG, f32)
    l_ref[...] = jnp.zeros(l_ref.shape, f32)
    acc_ref[...] = jnp.zeros(acc_ref.shape, f32)

    p_ref[1] = jnp.zeros(p_ref.shape[1:], bf16)
    al_ref[1] = jnp.ones(al_ref.shape[1:], f32)

    def kv_block(kb):
        return kv_ref[0, pl.ds(pl.multiple_of(kb * tk, tk), tk), :]

    def accumulate(kb, slot):
        pv = _dot(p_ref[slot], kv_block(kb))
        for j in range(KV_LATENT // LANES):
            js = slice(j * LANES, (j + 1) * LANES)
            acc_ref[:, js] = al_ref[slot] * acc_ref[:, js] + pv[:, js]

    def half_step(kb, cur, nxt):
        s_ref[nxt] = _dot(qs_ref[...], kv_block(jnp.minimum(kb + 1, nkb2 - 1)), _NT)
        accumulate(jnp.maximum(kb - 1, 0), nxt)
        c0 = pl.multiple_of(kb * tk, tk)
        bias_ref[...] = jnp.where(keys_ref[:, pl.ds(c0, tk)] >= thr_eff, 0.0, NEG_BIG)
        for h in range(ATT_HEADS):
            hs = slice(h * tq, (h + 1) * tq)
            mx = None
            for j in range(nl):
                js = slice(j * LANES, (j + 1) * LANES)
                t = s_ref[cur, hs, js] + bias_ref[:, js]
                s_ref[cur, hs, js] = t
                mx = t if mx is None else jnp.maximum(mx, t)
            m_prev = m_ref[hs, :]
            m_new = jnp.maximum(m_prev, jnp.max(mx, axis=1, keepdims=True))
            m_ref[hs, :] = m_new
            al_ref[cur, hs, :] = jnp.exp2(m_prev - m_new)
        for h in range(ATT_HEADS):
            hs = slice(h * tq, (h + 1) * tq)
            m_new = m_ref[hs, :]
            ps = None
            for j in range(nl):
                js = slice(j * LANES, (j + 1) * LANES)
                p = jnp.exp2(s_ref[cur, hs, js] - m_new)
                p_ref[cur, hs, js] = p.astype(bf16)
                ps = p if ps is None else ps + p
            l_ref[hs, :] = al_ref[cur, hs, :] * l_ref[hs, :] + ps

    def att_body(j, carry):
        half_step(2 * j, 0, 1)
        half_step(2 * j + 1, 1, 0)
        return carry

    s_ref[0] = _dot(qs_ref[...], kv_block(0), _NT)
    lax.fori_loop(0, nkb2 // 2, att_body, 0)
    accumulate(nkb2 - 1, 1)

    inv_l = 1.0 / jnp.sum(l_ref[...], axis=1, keepdims=True)
    o_lat = acc_ref[...] * inv_l
    for h in range(ATT_HEADS):
        y = _dot(o_lat[h * tq:(h + 1) * tq, :], wuv_ref[h])
        o_ref[0, :, h * ATT_HEAD_DIM:(h + 1) * ATT_HEAD_DIM] = y.astype(o_ref.dtype)


def _dsa_attention(q_lat, q_idx, small, k_idx, kv, w_uv, *, tq=128, tk=512):
    bsz, seq, _ = q_lat.shape
    topk = min(TOPK_MAX, seq // 4)
    tk = min(tk, seq)
    assert tk >= topk and seq % (2 * tk) == 0 and seq % tq == 0
    rows = ATT_HEADS * tq
    return pl.pallas_call(
        functools.partial(_attn_kernel, tq=tq, tk=tk, topk=topk, seq=seq),
        grid=(bsz, seq // tq),
        in_specs=[pl.BlockSpec((1, tq, ATT_HEADS * KV_LATENT), lambda b, i: (b, i, 0)),
                  pl.BlockSpec((1, tq, IDX_HEADS * IDX_DIM), lambda b, i: (b, i, 0)),
                  pl.BlockSpec((1, tq, LANES), lambda b, i: (b, i, 0)),
                  pl.BlockSpec((1, seq, IDX_DIM), lambda b, i: (b, 0, 0)),
                  pl.BlockSpec((1, seq, KV_LATENT), lambda b, i: (b, 0, 0)),
                  pl.BlockSpec((ATT_HEADS, KV_LATENT, ATT_HEAD_DIM), lambda b, i: (0, 0, 0))],
        out_specs=pl.BlockSpec((1, tq, ATT_HEADS * ATT_HEAD_DIM), lambda b, i: (b, i, 0)),
        out_shape=jax.ShapeDtypeStruct((bsz, seq, ATT_HEADS * ATT_HEAD_DIM), bf16),
        scratch_shapes=[pltpu.VMEM((tq, seq), i32),
                        pltpu.VMEM((rows, KV_LATENT), bf16),
                        pltpu.VMEM((rows, IDX_DIM), bf16),
                        pltpu.VMEM((2, rows, tk), f32),
                        pltpu.VMEM((tq, tk), f32),
                        pltpu.VMEM((2, rows, tk), bf16),
                        pltpu.VMEM((rows, KV_LATENT), f32),
                        pltpu.VMEM((rows, LANES), f32),
                        pltpu.VMEM((rows, LANES), f32),
                        pltpu.VMEM((2, rows, LANES), f32)],
        compiler_params=_cparams(("parallel", "arbitrary")),
        name="dsa_attention",
    )(q_lat, q_idx, small, k_idx, kv, w_uv)


def _dnprep_kernel(cur_ref, prev_ref, w_ref, q_ref, k_ref, v_ref, xx_ref, *, ts):
    i = pl.program_id(1)
    halo = SUBLANES
    xx_ref[0:halo, :] = jnp.where(i > 0, prev_ref[0], 0.0)
    xx_ref[halo:halo + ts, :] = cur_ref[0]
    for g in range(3 * DN_HEADS):
        cs = slice(g * LANES, (g + 1) * LANES)
        y = jnp.zeros((ts, LANES), f32)
        for j in range(CONV_WIDTH):
            off = halo - (CONV_WIDTH - 1) + j
            y = y + w_ref[j:j + 1, cs] * xx_ref[off:off + ts, cs]
        y = _silu(y)
        if g < 2 * DN_HEADS:
            y = y * lax.rsqrt(jnp.sum(y * y, axis=1, keepdims=True) + RMS_EPS)
        if g < DN_HEADS:
            q_ref[0, g] = y * (DN_DK ** -0.5)
        elif g < 2 * DN_HEADS:
            k_ref[0, g - DN_HEADS] = y
        else:
            v_ref[0, g - 2 * DN_HEADS] = y


def _dn_prep(qkv, conv_w, *, ts=256):
    bsz, seq, ch = qkv.shape
    ts = min(ts, seq)
    hb = ts // SUBLANES
    head_out = jax.ShapeDtypeStruct((bsz, DN_HEADS, seq, DN_DK), f32)
    head_spec = pl.BlockSpec((1, DN_HEADS, ts, DN_DK), lambda b, i: (b, 0, i, 0))
    return pl.pallas_call(
        functools.partial(_dnprep_kernel, ts=ts),
        grid=(bsz, seq // ts),
        in_specs=[pl.BlockSpec((1, ts, ch), lambda b, i: (b, i, 0)),
                  pl.BlockSpec((1, SUBLANES, ch), lambda b, i: (b, jnp.maximum(i * hb - 1, 0), 0)),
                  pl.BlockSpec((CONV_WIDTH, ch), lambda b, i: (0, 0))],
        out_specs=[head_spec, head_spec, head_spec],
        out_shape=[head_out, head_out, head_out],
        scratch_shapes=[pltpu.VMEM((ts + SUBLANES, ch), f32)],
        compiler_params=_cparams(("parallel", "parallel")),
        name="dn_prep",
    )(qkv, qkv, conv_w)


def _softplus(x):
    return jnp.maximum(x, 0.0) + jnp.log(1.0 + jnp.exp(-jnp.abs(x)))


def _dnchunk_kernel(alog_ref, dtb_ref, q_ref, k_ref, v_ref, small_ref, ar_ref,
                    z_ref, ng_ref, o_ref, state_ref, *, rows):
    t = pl.program_id(1)

    @pl.when(t == 0)
    def _():
        state_ref[...] = jnp.zeros(state_ref.shape, f32)

    ri = lax.broadcasted_iota(i32, (CHUNK, CHUNK), 0)
    ci = lax.broadcasted_iota(i32, (CHUNK, CHUNK), 1)
    tril = ri >= ci
    stril = ri > ci
    n_chunks = rows // CHUNK
    heads = range(DN_HEADS)
    pairs = [(h, c) for c in range(n_chunks) for h in heads]
    rows_of = lambda c: slice(c * CHUNK, (c + 1) * CHUNK)

    gcum_c, decay, kb = {}, {}, {}
    for p in pairs:
        h, c = p
        rs = rows_of(c)
        neg_a = -jnp.exp(jnp.zeros((1, 1), f32) + alog_ref[h])
        dtb = dtb_ref[h]
        g_col = neg_a * _softplus(small_ref[0, rs, SM_A + h:SM_A + h + 1] + dtb)
        g_row = neg_a * _softplus(ar_ref[0, h, :, rs] + dtb)
        gcum_c[p] = jnp.sum(jnp.where(tril, g_row, 0.0), axis=1, keepdims=True)
        gcum_r = jnp.sum(jnp.where(ri <= ci, g_col, 0.0), axis=0, keepdims=True)
        decay[p] = jnp.exp(jnp.where(tril, gcum_c[p] - gcum_r, -jnp.inf))
        kb[p] = k_ref[0, h, rs, :] * _sigmoid(small_ref[0, rs, SM_B + h:SM_B + h + 1])
    a = {p: jnp.where(stril, _dot(kb[p], k_ref[0, p[0], rows_of(p[1]), :], _NT) * decay[p], 0.0)
         for p in pairs}
    qk = {p: jnp.where(tril, _dot(q_ref[0, p[0], rows_of(p[1]), :],
                                  k_ref[0, p[0], rows_of(p[1]), :], _NT) * decay[p], 0.0)
          for p in pairs}
    r = {p: -a[p] for p in pairs}
    pw = a
    for _ in range(CHUNK.bit_length() - 2):
        pw = {p: _dot(pw[p], pw[p]) for p in pairs}
        r = {p: r[p] + pw[p] + _dot(r[p], pw[p]) for p in pairs}
    sol = {}
    for p in pairs:
        h, c = p
        rs = rows_of(c)
        beta = _sigmoid(small_ref[0, rs, SM_B + h:SM_B + h + 1])
        rhs = jnp.concatenate([v_ref[0, h, rs, :] * beta, kb[p] * jnp.exp(gcum_c[p])], axis=1)
        sol[p] = rhs + _dot(r[p], rhs)

    for c in range(n_chunks):
        rs = rows_of(c)
        state = {h: state_ref[h] for h in heads}
        v_new = {h: sol[(h, c)][:, :DN_DV] - _dot(sol[(h, c)][:, DN_DV:], state[h]) for h in heads}
        o_state = {h: _dot(q_ref[0, h, rs, :] * jnp.exp(gcum_c[(h, c)]), state[h]) for h in heads}
        o_local = {h: _dot(qk[(h, c)], v_new[h]) for h in heads}
        for h in heads:
            g = gcum_c[(h, c)]
            g_last = g[CHUNK - 1:CHUNK, :]
            k_tail = k_ref[0, h, rs, :] * jnp.exp(g_last - g)
            state_ref[h] = state[h] * jnp.exp(g_last) + _dot(k_tail, v_new[h], _TN)
        for h in heads:
            hc = slice(h * DN_DV, (h + 1) * DN_DV)
            o = o_state[h] + o_local[h]
            on = o * lax.rsqrt(jnp.mean(o * o, axis=1, keepdims=True) + RMS_EPS) * ng_ref[...]
            o_ref[0, rs, hc] = (on * _silu(z_ref[0, rs, hc])).astype(o_ref.dtype)


def _dn_chunk(q, k, v, small, a_row, z, a_log, dt_bias, norm_g, *, rows=128):
    bsz, heads, seq, dk = q.shape
    rows = min(rows, seq)
    hs = pl.BlockSpec((1, heads, rows, dk), lambda b, t, *_: (b, 0, t, 0))
    sms = pl.BlockSpec((1, rows, LANES), lambda b, t, *_: (b, t, 0))
    rsp = pl.BlockSpec((1, heads, 1, rows), lambda b, t, *_: (b, 0, 0, t))
    zs = pl.BlockSpec((1, rows, heads * DN_DV), lambda b, t, *_: (b, t, 0))
    return pl.pallas_call(
        functools.partial(_dnchunk_kernel, rows=rows),
        grid_spec=pltpu.PrefetchScalarGridSpec(
            num_scalar_prefetch=2,
            grid=(bsz, seq // rows),
            in_specs=[hs, hs, hs, sms, rsp, zs,
                      pl.BlockSpec((1, DN_DV), lambda b, t, *_: (0, 0))],
            out_specs=zs,
            scratch_shapes=[pltpu.VMEM((heads, DN_DK, DN_DV), f32)]),
        out_shape=jax.ShapeDtypeStruct((bsz, seq, heads * DN_DV), bf16),
        compiler_params=_cparams(("parallel", "arbitrary")),
        name="dn_chunk",
    )(a_log, dt_bias, q, k, v, small, a_row, z, norm_g.reshape(1, DN_DV))


def _layer_norm(r, g, b):
    mu = jnp.mean(r, axis=-1, keepdims=True)
    d = r - mu
    var = jnp.mean(d * d, axis=-1, keepdims=True)
    return d * lax.rsqrt(var + LN_EPS) * g + b


def _lane_min_index(hit, lane):
    return jnp.min(jnp.where(hit, lane, float(LANES)), axis=1, keepdims=True)


def _merge_kernel(ya_ref, yd_ref, ga_ref, gd_ref, x_ref, gt_ref, lng_ref, lnb_ref, sc_ref, sh_ref,
                  wa_ref, wd_ref, wo_ref, wr_ref, x1_ref, h2_ref, eidx_ref, gate_ref, *, alpha):
    merged = (ga_ref[0] * _dot(ya_ref[0], wa_ref[...]) + gd_ref[0] * _dot(yd_ref[0], wd_ref[...]))
    y = _dot(merged, wo_ref[...])
    x1 = _layer_norm(alpha * x_ref[0] + (1.0 + gt_ref[0]) * y, lng_ref[...], lnb_ref[...])
    x1_ref[0] = x1
    h2 = x1 * (1.0 + sc_ref[0]) + sh_ref[0]
    h2_ref[0] = h2
    logits = _dot(h2, wr_ref[...], exact=True)
    lane = lax.broadcasted_iota(i32, logits.shape, 1).astype(f32)
    lg = jnp.where(lane < N_GROUPS, logits, -jnp.inf)
    mg = jnp.max(lg, axis=1, keepdims=True)
    top_gp = 1.0 / jnp.sum(jnp.exp(lg - mg), axis=1, keepdims=True)
    g_idx = _lane_min_index(lg == mg, lane)
    lo = N_GROUPS + g_idx * EXPERTS_PER_GROUP
    in_grp = jnp.logical_and(lane >= lo, lane < lo + EXPERTS_PER_GROUP)
    le = jnp.where(in_grp, logits, -jnp.inf)
    m1 = jnp.max(le, axis=1, keepdims=True)
    i1 = _lane_min_index(le == m1, lane)
    le2 = jnp.where(lane == i1, -jnp.inf, le)
    m2 = jnp.max(le2, axis=1, keepdims=True)
    i2 = _lane_min_index(le2 == m2, lane)
    e2 = jnp.exp(m2 - m1)
    gate1 = top_gp / (1.0 + e2)
    gate2 = top_gp * e2 / (1.0 + e2)
    e_lanes = jnp.where(lane == 0.0, i1 - N_GROUPS, jnp.where(lane == 1.0, i2 - N_GROUPS, 0.0))
    eidx_ref[0] = e_lanes.astype(i32)
    gate_ref[0] = jnp.where(lane == 0.0, gate1, jnp.where(lane == 1.0, gate2, 0.0))


def _merge(y_att, y_dn, gates, x, gt1, ln_g, ln_b, sc2, sh2, w_br_att, w_br_dn, w_out, w_route,
           alpha, *, tm=256):
    bsz, seq, d = x.shape
    tm = min(tm, seq)
    row = lambda b, i: (b, i, 0)
    per_b = pl.BlockSpec((1, 1, d), lambda b, i: (b, 0, 0))
    vec = pl.BlockSpec((1, d), lambda b, i: (0, 0))
    wsp = pl.BlockSpec((d, d), lambda b, i: (0, 0))
    out_f = jax.ShapeDtypeStruct((bsz, seq, d), f32)
    return pl.pallas_call(
        functools.partial(_merge_kernel, alpha=alpha),
        grid=(bsz, seq // tm),
        in_specs=[pl.BlockSpec((1, tm, d), row), pl.BlockSpec((1, tm, d), row),
                  pl.BlockSpec((1, tm, d), lambda b, i: (b, i, 0)),
                  pl.BlockSpec((1, tm, d), lambda b, i: (b, i, 1)),
                  pl.BlockSpec((1, tm, d), row), per_b, vec, vec, per_b, per_b,
                  wsp, wsp, wsp, pl.BlockSpec((d, LANES), lambda b, i: (0, 0))],
        out_specs=[pl.BlockSpec((1, tm, d), row), pl.BlockSpec((1, tm, d), row),
                   pl.BlockSpec((1, tm, LANES), row), pl.BlockSpec((1, tm, LANES), row)],
        out_shape=[out_f, out_f, jax.ShapeDtypeStruct((bsz, seq, LANES), i32),
                   jax.ShapeDtypeStruct((bsz, seq, LANES), f32)],
        compiler_params=_cparams(("parallel", "parallel")),
        name="merge_router",
    )(y_att, y_dn, gates, gates, x, gt1, ln_g, ln_b, sc2, sh2, w_br_att, w_br_dn, w_out, w_route)


def _expert_kernel(blk_e_ref, tok_ref, slot_ref, h_hbm, wg_ref, wu_ref, wd_ref, y_hbm,
                   xbuf, ybuf, gsem, ssem):
    i = pl.program_id(0)
    nb = pl.num_programs(0)
    cur = i % 2

    def gather_copy(blk, buf, r):
        tok = tok_ref[blk * MOE_BLOCK + r]
        return pltpu.make_async_copy(h_hbm.at[pl.ds(tok, 1), :], xbuf.at[buf, pl.ds(r, 1), :],
                                     gsem.at[buf])

    def scatter_copy(blk, buf, r):
        slot = slot_ref[blk * MOE_BLOCK + r]
        return pltpu.make_async_copy(ybuf.at[buf, pl.ds(r, 1), :], y_hbm.at[pl.ds(slot, 1), :],
                                     ssem.at[buf])

    def start_rows(copy_fn, blk, buf):
        def body(r, carry):
            copy_fn(blk, buf, r).start()
            return carry
        lax.fori_loop(0, MOE_BLOCK, body, 0, unroll=8)

    def wait_gather(buf):
        pltpu.make_async_copy(h_hbm.at[pl.ds(0, MOE_BLOCK), :], xbuf.at[buf], gsem.at[buf]).wait()

    def wait_scatter(buf):
        pltpu.make_async_copy(ybuf.at[buf], y_hbm.at[pl.ds(0, MOE_BLOCK), :], ssem.at[buf]).wait()

    def start_rows_inline(copy_fn, blk, buf):
        for r in range(MOE_BLOCK):
            copy_fn(blk, buf, r).start()

    def ffn(buf):
        xb = xbuf[buf]
        hb = _silu(_dot(xb, wg_ref[0])) * _dot(xb, wu_ref[0])
        return _dot(hb, wd_ref[0])

    @pl.when(i == 0)
    def _():
        start_rows(gather_copy, 0, 0)

    wait_gather(cur)

    @pl.when(i >= 2)
    def _():
        wait_scatter(cur)

    @pl.when(i == 0)
    def _():
        start_rows_inline(gather_copy, i + 1, 1 - cur)
        ybuf[cur] = ffn(cur)

    @pl.when(jnp.logical_and(i > 0, i < nb - 1))
    def _():
        start_rows_inline(gather_copy, i + 1, 1 - cur)
        start_rows_inline(scatter_copy, i - 1, 1 - cur)
        ybuf[cur] = ffn(cur)

    @pl.when(i == nb - 1)
    def _():
        start_rows_inline(scatter_copy, i - 1, 1 - cur)
        ybuf[cur] = ffn(cur)
        start_rows(scatter_copy, i, cur)
        wait_scatter(1 - cur)
        wait_scatter(cur)


def _experts(h2, blk_e, tok_buf, slot_buf, w_gate, w_up, w_down, n_slots):
    n_tok, d = h2.shape
    nb = blk_e.shape[0]
    ff = w_gate.shape[-1]
    return pl.pallas_call(
        _expert_kernel,
        grid_spec=pltpu.PrefetchScalarGridSpec(
            num_scalar_prefetch=3,
            grid=(nb,),
            in_specs=[pl.BlockSpec(memory_space=pl.ANY),
                      pl.BlockSpec((1, d, ff), lambda i, be, *_: (be[i], 0, 0)),
                      pl.BlockSpec((1, d, ff), lambda i, be, *_: (be[i], 0, 0)),
                      pl.BlockSpec((1, ff, d), lambda i, be, *_: (be[i], 0, 0))],
            out_specs=pl.BlockSpec(memory_space=pl.ANY),
            scratch_shapes=[pltpu.VMEM((2, MOE_BLOCK, d), f32),
                            pltpu.VMEM((2, MOE_BLOCK, d), f32),
                            pltpu.SemaphoreType.DMA((2,)),
                            pltpu.SemaphoreType.DMA((2,))]),
        out_shape=jax.ShapeDtypeStruct((n_slots, d), f32),
        compiler_params=_cparams(("arbitrary",)),
        name="moe_experts",
    )(blk_e, tok_buf, slot_buf, h2, w_gate, w_up, w_down)


def _route_plan(e_idx, n_tok):
    m = n_tok * EXPERT_TOPK
    flat_e = e_idx.reshape(m)
    order = jnp.argsort(flat_e, stable=True).astype(i32)
    experts = jnp.arange(N_EXPERTS, dtype=i32)
    counts = jnp.sum((flat_e[:, None] == experts[None, :]).astype(i32), axis=0)
    pcounts = (counts + MOE_BLOCK - 1) // MOE_BLOCK * MOE_BLOCK
    starts = jnp.cumsum(counts) - counts
    pends = jnp.cumsum(pcounts)
    pstarts = pends - pcounts
    nb = -(-m // MOE_BLOCK) + N_EXPERTS
    rows = nb * MOE_BLOCK
    blk_raw = jnp.sum((pends[None, :] <= (jnp.arange(nb, dtype=i32) * MOE_BLOCK)[:, None]).astype(i32),
                      axis=1)
    blk_e = jnp.minimum(blk_raw, N_EXPERTS - 1)
    pos = jnp.arange(rows, dtype=i32)
    in_region = jnp.repeat(blk_raw, MOE_BLOCK) < N_EXPERTS
    e_row = jnp.repeat(blk_e, MOE_BLOCK)
    rank = pos - pstarts[e_row]
    valid = jnp.logical_and(in_region, rank < counts[e_row])
    assign = order[jnp.clip(starts[e_row] + rank, 0, m - 1)]
    n_before = jnp.where(in_region, starts[e_row] + jnp.minimum(rank, counts[e_row]), m)
    slot = (assign % EXPERT_TOPK) * n_tok + assign // EXPERT_TOPK
    slot_buf = jnp.where(valid, slot, m + pos - n_before)
    tok_buf = jnp.where(valid, assign // EXPERT_TOPK, 0)
    return blk_e, tok_buf, slot_buf, rows - m


def _final_kernel(y0_ref, y1_ref, g_ref, x_ref, gt_ref, lng_ref, lnb_ref, o_ref, *, alpha):
    y = y0_ref[...] * g_ref[0, :, 0:1] + y1_ref[...] * g_ref[0, :, 1:2]
    o_ref[0] = _layer_norm(alpha * x_ref[0] + (1.0 + gt_ref[0]) * y, lng_ref[...], lnb_ref[...])


def _final(y_slots, g_lanes, x, gt2, ln_g, ln_b, alpha, *, tm=512):
    bsz, seq, d = x.shape
    tm = min(tm, seq)
    nblk = seq // tm
    second = bsz * nblk
    per_b = pl.BlockSpec((1, 1, d), lambda b, i: (b, 0, 0))
    vec = pl.BlockSpec((1, d), lambda b, i: (0, 0))
    return pl.pallas_call(
        functools.partial(_final_kernel, alpha=alpha),
        grid=(bsz, nblk),
        in_specs=[pl.BlockSpec((tm, d), lambda b, i: (b * nblk + i, 0)),
                  pl.BlockSpec((tm, d), lambda b, i: (second + b * nblk + i, 0)),
                  pl.BlockSpec((1, tm, LANES), lambda b, i: (b, i, 0)),
                  pl.BlockSpec((1, tm, d), lambda b, i: (b, i, 0)), per_b, vec, vec],
        out_specs=pl.BlockSpec((1, tm, d), lambda b, i: (b, i, 0)),
        out_shape=jax.ShapeDtypeStruct((bsz, seq, d), f32),
        compiler_params=_cparams(("parallel", "parallel")),
        name="moe_combine_ln",
    )(y_slots, y_slots, g_lanes, x, gt2, ln_g, ln_b)


def _pack_w_in(w_in):
    sizes = (ATT_HEADS * ATT_HEAD_DIM, KV_LATENT, IDX_HEADS * IDX_DIM, IDX_DIM, IDX_HEADS,
             3 * DN_HEADS * DN_DK, DN_HEADS, DN_HEADS, DN_HEADS * DN_DV, 2 * w_in.shape[0])
    offs = [0]
    for s in sizes:
        offs.append(offs[-1] + s)
    w_in = w_in.astype(bf16)
    seg = [w_in[:, offs[j]:offs[j + 1]] for j in range(len(sizes))]
    w_q, w_ckv, w_qidx, w_kidx, w_widx, w_qkv, w_a, w_b, w_z, w_gates = seg
    pad = jnp.zeros((w_in.shape[0], LANES - (IDX_DIM + IDX_HEADS + 2 * DN_HEADS)), w_in.dtype)
    packed = jnp.concatenate([w_q, w_ckv, w_qidx, w_kidx, w_widx, w_a, w_b, pad, w_qkv, w_z, w_gates],
                             axis=1)
    return packed


def kernel(x, c, w_ada, b_ada, w_in, kv_norm_g, w_uk, w_uv, conv_w, a_log, dt_bias, dn_norm_g,
           w_br_att, w_br_dn, w_out, w_route_grp, w_route_exp, w_gate, w_up, w_down, ln_g, ln_b):
    depth = w_in.shape[0]
    bsz, seq, d = x.shape
    n_tok = bsz * seq
    alpha = (2.0 * depth) ** 0.25
    mod = _ada(c, w_ada, b_ada)
    for l in range(depth):
        sh1, sc1, gt1, sh2, sc2, gt2 = [mod[l, :, j * d:(j + 1) * d].reshape(bsz, 1, d)
                                        for j in range(6)]
        q_lat, kv, q_idx, small, qkv, z, gates, k_idx = _proj(
            x, sc1, sh1, _pack_w_in(w_in[l]), jnp.swapaxes(w_uk[l], 1, 2).astype(bf16), kv_norm_g[l])

        y_att = _dsa_attention(q_lat, q_idx, small, k_idx, kv, w_uv[l].astype(bf16))

        dq, dk, dv = _dn_prep(qkv, conv_w[l])
        a_row = jnp.swapaxes(small[..., SM_A:SM_A + DN_HEADS], 1, 2)[:, :, None, :]
        y_dn = _dn_chunk(dq, dk, dv, small, a_row, z, a_log[l], dt_bias[l], dn_norm_g[l])

        w_route = jnp.zeros((d, LANES), f32)
        w_route = w_route.at[:, :N_GROUPS].set(w_route_grp[l])
        w_route = w_route.at[:, N_GROUPS:N_GROUPS + N_EXPERTS].set(w_route_exp[l])
        x1, h2, e_lanes, g_lanes = _merge(
            y_att, y_dn, gates, x, gt1, ln_g[l, 0].reshape(1, d), ln_b[l, 0].reshape(1, d),
            sc2, sh2, w_br_att[l].astype(bf16), w_br_dn[l].astype(bf16), w_out[l].astype(bf16),
            w_route, alpha)

        e_idx = e_lanes.reshape(n_tok, LANES)[:, :EXPERT_TOPK]
        blk_e, tok_buf, slot_buf, n_pad = _route_plan(e_idx, n_tok)
        y_slots = _experts(h2.reshape(n_tok, d), blk_e, tok_buf, slot_buf,
                           w_gate[l], w_up[l], w_down[l],
                           n_tok * EXPERT_TOPK + n_pad)
        x = _final(y_slots, g_lanes, x1, gt2, ln_g[l, 1].reshape(1, d), ln_b[l, 1].reshape(1, d), alpha)
    return x
```

```python
import functools

import jax
import jax.numpy as jnp
from jax import lax
from jax.experimental import pallas as pl
from jax.experimental.pallas import tpu as pltpu

f32 = jnp.float32
bf16 = jnp.bfloat16
i32 = jnp.int32

ATT_HEADS = 8
ATT_HEAD_DIM = 128
KV_LATENT = 256
IDX_HEADS = 8
IDX_DIM = 64
TOPK_MAX = 256
DN_HEADS = 8
DN_DK = 128
DN_DV = 128
CONV_WIDTH = 4
CHUNK = 64
N_GROUPS = 4
EXPERTS_PER_GROUP = 8
N_EXPERTS = N_GROUPS * EXPERTS_PER_GROUP
EXPERT_TOPK = 2
EXPERT_FF = 512
MOE_BLOCK = 128
GATHER_SLOTS = 3
LN_EPS = 1e-5
RMS_EPS = 1e-6
ATT_SCALE = ATT_HEAD_DIM ** -0.5
INDEX_SCALE = (IDX_HEADS ** -0.5) * (IDX_DIM ** -0.5)
LOG2_E = 1.4426950408889634

LANES = 128
SUBLANES = 8
VMEM_LIMIT = 56 * 1024 * 1024
INT_MIN = -(2 ** 31)
NEG_BIG = -1e30

SM_KIDX = 0
SM_WIDX = IDX_DIM
SM_A = SM_WIDX + IDX_HEADS
SM_B = SM_A + DN_HEADS

_NT = (((1,), (1,)), ((), ()))
_TN = (((0,), (0,)), ((), ()))


def _dot(a, b, dims=None, exact=False):
    if dims is None:
        dims = (((a.ndim - 1,), (0,)), ((), ()))
    if exact:
        return lax.dot_general(a.astype(f32), b.astype(f32), dims,
                               precision=lax.Precision.HIGHEST, preferred_element_type=f32)
    return lax.dot_general(a.astype(bf16), b.astype(bf16), dims, preferred_element_type=f32)


def _sigmoid(x):
    return 1.0 / (1.0 + jnp.exp(-x))


def _silu(x):
    return x * _sigmoid(x)


def _cparams(sem):
    return pltpu.CompilerParams(dimension_semantics=sem, vmem_limit_bytes=VMEM_LIMIT)


def _ada_kernel(c_ref, w_ref, b_ref, o_ref):
    cond = _silu(c_ref[...])
    o_ref[0] = _dot(cond, w_ref[0], exact=True) + b_ref[0]


def _ada(c, w_ada, b_ada):
    depth, d, n = w_ada.shape
    b = c.shape[0]
    rows = max(SUBLANES, -(-b // SUBLANES) * SUBLANES)
    cp = jnp.zeros((rows, d), f32).at[:b].set(c)
    tn = 1536
    out = pl.pallas_call(
        _ada_kernel,
        grid=(depth, n // tn),
        in_specs=[pl.BlockSpec((rows, d), lambda l, j: (0, 0)),
                  pl.BlockSpec((1, d, tn), lambda l, j: (l, 0, j)),
                  pl.BlockSpec((1, 1, tn), lambda l, j: (l, 0, j))],
        out_specs=pl.BlockSpec((1, rows, tn), lambda l, j: (l, 0, j)),
        out_shape=jax.ShapeDtypeStruct((depth, rows, n), f32),
        compiler_params=_cparams(("parallel", "parallel")),
        name="ada_mod",
    )(cp, w_ada, b_ada.reshape(depth, 1, n))
    return out[:, :b]


PW_Q = 0
PW_CKV = PW_Q + ATT_HEADS * ATT_HEAD_DIM
PW_QIDX = PW_CKV + KV_LATENT
PW_SMALL = PW_QIDX + IDX_HEADS * IDX_DIM
PW_QKV = PW_SMALL + LANES
PW_Z = PW_QKV + 3 * DN_HEADS * DN_DK
PW_GATES = PW_Z + DN_HEADS * DN_DV
PROJ_TN = 1024


def _proj_kernel(x_ref, sc_ref, sh_ref, w_ref, wuk_ref, g_ref,
                 qlat_ref, kv_ref, qidx_ref, small_ref, qkv_ref, z_ref, gates_ref, kidx_ref):
    h = (x_ref[0] * (1.0 + sc_ref[0]) + sh_ref[0]).astype(bf16)

    def cols(start, width):
        return _dot(h, w_ref[:, start:start + width])

    q_att = cols(PW_Q, ATT_HEADS * ATT_HEAD_DIM)
    for j in range(ATT_HEADS):
        ql = _dot(q_att[:, j * ATT_HEAD_DIM:(j + 1) * ATT_HEAD_DIM], wuk_ref[j]) * (ATT_SCALE * LOG2_E)
        qlat_ref[0, :, j * KV_LATENT:(j + 1) * KV_LATENT] = ql.astype(qlat_ref.dtype)
    c_kv = cols(PW_CKV, KV_LATENT)
    c_kv = c_kv * lax.rsqrt(jnp.mean(c_kv * c_kv, axis=-1, keepdims=True) + RMS_EPS)
    kv_ref[0] = (c_kv * g_ref[...]).astype(kv_ref.dtype)
    qidx_ref[0] = cols(PW_QIDX, IDX_HEADS * IDX_DIM).astype(qidx_ref.dtype)
    small = cols(PW_SMALL, LANES)
    small_ref[0] = small
    kidx_ref[0] = small[:, SM_KIDX:SM_KIDX + IDX_DIM].astype(kidx_ref.dtype)
    for c in range(qkv_ref.shape[-1] // PROJ_TN):
        qkv_ref[0, :, c * PROJ_TN:(c + 1) * PROJ_TN] = cols(PW_QKV + c * PROJ_TN, PROJ_TN)
    z_ref[0] = cols(PW_Z, DN_HEADS * DN_DV)
    for c in range(gates_ref.shape[-1] // PROJ_TN):
        gates_ref[0, :, c * PROJ_TN:(c + 1) * PROJ_TN] = _sigmoid(cols(PW_GATES + c * PROJ_TN, PROJ_TN))


def _proj(x, sc, sh, w_packed, w_ukt, kv_norm_g, *, tm=256):
    bsz, seq, d = x.shape
    tm = min(tm, seq)
    n = w_packed.shape[1]
    widths = (ATT_HEADS * KV_LATENT, KV_LATENT, IDX_HEADS * IDX_DIM, LANES, 3 * DN_HEADS * DN_DK,
              DN_HEADS * DN_DV, 2 * d, IDX_DIM)
    dtypes = (bf16, bf16, bf16, f32, f32, f32, f32, bf16)
    row = lambda b, i: (b, i, 0)
    per_b = pl.BlockSpec((1, 1, d), lambda b, i: (b, 0, 0))
    once = pl.Buffered(1)
    return pl.pallas_call(
        _proj_kernel,
        grid=(bsz, seq // tm),
        in_specs=[pl.BlockSpec((1, tm, d), row), per_b, per_b,
                  pl.BlockSpec((d, n), lambda b, i: (0, 0), pipeline_mode=once),
                  pl.BlockSpec(w_ukt.shape, lambda b, i: (0, 0, 0), pipeline_mode=once),
                  pl.BlockSpec((1, KV_LATENT), lambda b, i: (0, 0))],
        out_specs=[pl.BlockSpec((1, tm, wd), row) for wd in widths],
        out_shape=[jax.ShapeDtypeStruct((bsz, seq, wd), dt) for wd, dt in zip(widths, dtypes)],
        compiler_params=_cparams(("parallel", "parallel")),
        name="proj_in",
    )(x, sc, sh, w_packed, w_ukt, kv_norm_g.reshape(1, KV_LATENT))


def _attn_kernel(qlat_ref, qidx_ref, small_ref, kidx_ref, kv_ref, wuv_ref, o_ref,
                 keys_ref, qs_ref, qis_ref, s_ref, bias_ref, p_ref, acc_ref, m_ref, l_ref, al_ref,
                 *, tq, tk, topk, seq):
    i = pl.program_id(1)
    q0 = i * tq
    nkb = (q0 + tq + tk - 1) // tk
    nkb2 = nkb + (nkb & 1)
    int_min = jnp.int32(INT_MIN)
    row = q0 + lax.broadcasted_iota(i32, (tq, tk), 0)
    col = lax.broadcasted_iota(i32, (tq, tk), 1)
    lane = lax.broadcasted_iota(i32, (tq, LANES), 1)
    nl = tk // LANES

    for h in range(IDX_HEADS):
        qis_ref[h * tq:(h + 1) * tq, :] = qidx_ref[0, :, h * IDX_DIM:(h + 1) * IDX_DIM]
    wts = small_ref[0][:, SM_WIDX:SM_WIDX + IDX_HEADS] * INDEX_SCALE

    def idx_scores(kb):
        kblk = kidx_ref[0, pl.ds(pl.multiple_of(kb * tk, tk), tk), :]
        return _dot(qis_ref[...], kblk, _NT)

    def score_half(kb, cur, nxt):
        s_ref[nxt] = idx_scores(jnp.minimum(kb + 1, nkb2 - 1))
        c0 = pl.multiple_of(kb * tk, tk)
        acc = jnp.zeros((tq, tk), f32)
        for h in range(IDX_HEADS):
            acc = acc + jnp.maximum(s_ref[cur, h * tq:(h + 1) * tq, :], 0.0) * wts[:, h:h + 1]
        bits = pltpu.bitcast(acc, i32)
        key = bits ^ ((bits >> 31) & jnp.int32(0x7FFFFFFF))
        key = jnp.where(c0 + col <= row, key, int_min)
        keys_ref[:, pl.ds(c0, tk)] = key

    def score_body(j, carry):
        score_half(2 * j, 0, 1)
        score_half(2 * j + 1, 1, 0)
        return carry

    s_ref[0] = idx_scores(0)
    lax.fori_loop(0, nkb2 // 2, score_body, 0)

    def count(pred):
        def body(kb, part):
            c0 = pl.multiple_of(kb * tk, tk)
            blk = keys_ref[:, pl.ds(c0, tk)]
            for j in range(nl):
                part = part + pred(blk[:, j * LANES:(j + 1) * LANES], c0 + j * LANES + lane)
            return part
        part = lax.fori_loop(0, nkb, body, jnp.zeros((tq, LANES), f32))
        return jnp.sum(part, axis=1, keepdims=True)

    kf = jnp.float32(topk)

    def radix_body(it, carry):
        thr, cnt_thr = carry
        cand = thr + (jnp.int32(1) << (31 - it))
        cand_b = jnp.broadcast_to(cand, (tq, LANES))
        cnt = count(lambda k, c: jnp.where(k >= cand_b, 1.0, 0.0))
        ok = cnt >= kf
        return jnp.where(ok, cand, thr), jnp.where(ok, cnt, cnt_thr)

    thr0 = jnp.full((tq, 1), INT_MIN, i32)
    cnt0 = jnp.zeros((tq, 1), f32) + (nkb * tk).astype(f32)
    thr, cnt_thr = lax.fori_loop(0, 32, radix_body, (thr0, cnt0))

    need = jnp.logical_and(cnt_thr > kf, thr > int_min)
    any_need = jnp.max(jnp.where(need, 1.0, 0.0)) > 0.0

    @pl.when(any_need)
    def _():
        thr_b = jnp.broadcast_to(thr, (tq, LANES))
        n_gt = count(lambda k, c: jnp.where(k > thr_b, 1.0, 0.0))
        quota = kf - n_gt

        def cut_body(it, cut):
            cand = cut + (jnp.int32(1) << (seq.bit_length() - 1 - it))
            cand_b = jnp.broadcast_to(cand, (tq, LANES))
            cnt = count(lambda k, c: jnp.where(k == thr_b, jnp.where(c < cand_b, 1.0, 0.0), 0.0))
            return jnp.where(cnt <= quota, cand, cut)

        cut = lax.fori_loop(0, seq.bit_length(), cut_body, jnp.zeros((tq, 1), i32))

        def drop_body(kb, carry):
            c0 = pl.multiple_of(kb * tk, tk)
            blk = keys_ref[:, pl.ds(c0, tk)]
            dropped = jnp.where(c0 + col >= cut, int_min, blk)
            keys_ref[:, pl.ds(c0, tk)] = jnp.where(blk == thr, dropped, blk)
            return carry

        lax.fori_loop(0, nkb, drop_body, 0)

    thr_eff = jnp.maximum(thr, int_min + 1)
    for h in range(ATT_HEADS):
        qs_ref[h * tq:(h + 1) * tq, :] = qlat_ref[0, :, h * KV_LATENT:(h + 1) * KV_LATENT]
    m_ref[...] = jnp.full(m_ref.shape, NEG_BIG, f32)
    l_ref[...] = jnp.zeros(l_ref.shape, f32)
    acc_ref[...] = jnp.zeros(acc_ref.shape, f32)

    p_ref[1] = jnp.zeros(p_ref.shape[1:], bf16)
    al_ref[1] = jnp.ones(al_ref.shape[1:], f32)

    def kv_block(kb):
        return kv_ref[0, pl.ds(pl.multiple_of(kb * tk, tk), tk), :]

    def accumulate(kb, slot):
        pv = _dot(p_ref[slot], kv_block(kb))
        for j in range(KV_LATENT // LANES):
            js = slice(j * LANES, (j + 1) * LANES)
            acc_ref[:, js] = al_ref[slot] * acc_ref[:, js] + pv[:, js]

    def half_step(kb, cur, nxt):
        s_ref[nxt] = _dot(qs_ref[...], kv_block(jnp.minimum(kb + 1, nkb2 - 1)), _NT)
        accumulate(jnp.maximum(kb - 1, 0), nxt)
        c0 = pl.multiple_of(kb * tk, tk)
        bias_ref[...] = jnp.where(keys_ref[:, pl.ds(c0, tk)] >= thr_eff, 0.0, NEG_BIG)
        for h in range(ATT_HEADS):
            hs = slice(h * tq, (h + 1) * tq)
            mx = None
            for j in range(nl):
                js = slice(j * LANES, (j + 1) * LANES)
                t = s_ref[cur, hs, js] + bias_ref[:, js]
                s_ref[cur, hs, js] = t
                mx = t if mx is None else jnp.maximum(mx, t)
            m_prev = m_ref[hs, :]
            m_new = jnp.maximum(m_prev, jnp.max(mx, axis=1, keepdims=True))
            m_ref[hs, :] = m_new
            al_ref[cur, hs, :] = jnp.exp2(m_prev - m_new)
        for h in range(ATT_HEADS):
            hs = slice(h * tq, (h + 1) * tq)
            m_new = m_ref[hs, :]
            ps = None
            for j in range(nl):
                js = slice(j * LANES, (j + 1) * LANES)
                p = jnp.exp2(s_ref[cur, hs, js] - m_new)
                p_ref[cur, hs, js] = p.astype(bf16)
                ps = p if ps is None else ps + p
            l_ref[hs, :] = al_ref[cur, hs, :] * l_ref[hs, :] + ps

    def att_body(j, carry):
        half_step(2 * j, 0, 1)
        half_step(2 * j + 1, 1, 0)
        return carry

    s_ref[0] = _dot(qs_ref[...], kv_block(0), _NT)
    lax.fori_loop(0, nkb2 // 2, att_body, 0)
    accumulate(nkb2 - 1, 1)

    inv_l = 1.0 / jnp.sum(l_ref[...], axis=1, keepdims=True)
    o_lat = acc_ref[...] * inv_l
    for h in range(ATT_HEADS):
        y = _dot(o_lat[h * tq:(h + 1) * tq, :], wuv_ref[h])
        o_ref[0, :, h * ATT_HEAD_DIM:(h + 1) * ATT_HEAD_DIM] = y.astype(o_ref.dtype)


def _dsa_attention(q_lat, q_idx, small, k_idx, kv, w_uv, *, tq=128, tk=512):
    bsz, seq, _ = q_lat.shape
    topk = min(TOPK_MAX, seq // 4)
    tk = min(tk, seq)
    assert tk >= topk and seq % (2 * tk) == 0 and seq % tq == 0
    rows = ATT_HEADS * tq
    return pl.pallas_call(
        functools.partial(_attn_kernel, tq=tq, tk=tk, topk=topk, seq=seq),
        grid=(bsz, seq // tq),
        in_specs=[pl.BlockSpec((1, tq, ATT_HEADS * KV_LATENT), lambda b, i: (b, i, 0)),
                  pl.BlockSpec((1, tq, IDX_HEADS * IDX_DIM), lambda b, i: (b, i, 0)),
                  pl.BlockSpec((1, tq, LANES), lambda b, i: (b, i, 0)),
                  pl.BlockSpec((1, seq, IDX_DIM), lambda b, i: (b, 0, 0)),
                  pl.BlockSpec((1, seq, KV_LATENT), lambda b, i: (b, 0, 0)),
                  pl.BlockSpec((ATT_HEADS, KV_LATENT, ATT_HEAD_DIM), lambda b, i: (0, 0, 0))],
        out_specs=pl.BlockSpec((1, tq, ATT_HEADS * ATT_HEAD_DIM), lambda b, i: (b, i, 0)),
        out_shape=jax.ShapeDtypeStruct((bsz, seq, ATT_HEADS * ATT_HEAD_DIM), bf16),
        scratch_shapes=[pltpu.VMEM((tq, seq), i32),
                        pltpu.VMEM((rows, KV_LATENT), bf16),
                        pltpu.VMEM((rows, IDX_DIM), bf16),
                        pltpu.VMEM((2, rows, tk), f32),
                        pltpu.VMEM((tq, tk), f32),
                        pltpu.VMEM((2, rows, tk), bf16),
                        pltpu.VMEM((rows, KV_LATENT), f32),
                        pltpu.VMEM((rows, LANES), f32),
                        pltpu.VMEM((rows, LANES), f32),
                        pltpu.VMEM((2, rows, LANES), f32)],
        compiler_params=_cparams(("parallel", "arbitrary")),
        name="dsa_attention",
    )(q_lat, q_idx, small, k_idx, kv, w_uv)


def _dnprep_kernel(cur_ref, prev_ref, w_ref, q_ref, k_ref, v_ref, xx_ref, *, ts):
    i = pl.program_id(1)
    halo = SUBLANES
    xx_ref[0:halo, :] = jnp.where(i > 0, prev_ref[0], 0.0)
    xx_ref[halo:halo + ts, :] = cur_ref[0]
    for g in range(3 * DN_HEADS):
        cs = slice(g * LANES, (g + 1) * LANES)
        y = jnp.zeros((ts, LANES), f32)
        for j in range(CONV_WIDTH):
            off = halo - (CONV_WIDTH - 1) + j
            y = y + w_ref[j:j + 1, cs] * xx_ref[off:off + ts, cs]
        y = _silu(y)
        if g < 2 * DN_HEADS:
            y = y * lax.rsqrt(jnp.sum(y * y, axis=1, keepdims=True) + RMS_EPS)
        if g < DN_HEADS:
            q_ref[0, g] = y * (DN_DK ** -0.5)
        elif g < 2 * DN_HEADS:
            k_ref[0, g - DN_HEADS] = y
        else:
            v_ref[0, g - 2 * DN_HEADS] = y


def _dn_prep(qkv, conv_w, *, ts=256):
    bsz, seq, ch = qkv.shape
    ts = min(ts, seq)
    hb = ts // SUBLANES
    head_out = jax.ShapeDtypeStruct((bsz, DN_HEADS, seq, DN_DK), f32)
    head_spec = pl.BlockSpec((1, DN_HEADS, ts, DN_DK), lambda b, i: (b, 0, i, 0))
    return pl.pallas_call(
        functools.partial(_dnprep_kernel, ts=ts),
        grid=(bsz, seq // ts),
        in_specs=[pl.BlockSpec((1, ts, ch), lambda b, i: (b, i, 0)),
                  pl.BlockSpec((1, SUBLANES, ch), lambda b, i: (b, jnp.maximum(i * hb - 1, 0), 0)),
                  pl.BlockSpec((CONV_WIDTH, ch), lambda b, i: (0, 0))],
        out_specs=[head_spec, head_spec, head_spec],
        out_shape=[head_out, head_out, head_out],
        scratch_shapes=[pltpu.VMEM((ts + SUBLANES, ch), f32)],
        compiler_params=_cparams(("parallel", "parallel")),
        name="dn_prep",
    )(qkv, qkv, conv_w)


def _softplus(x):
    return jnp.maximum(x, 0.0) + jnp.log(1.0 + jnp.exp(-jnp.abs(x)))


def _dnchunk_kernel(alog_ref, dtb_ref, q_ref, k_ref, v_ref, small_ref, ar_ref,
                    z_ref, ng_ref, o_ref, state_ref, *, rows):
    t = pl.program_id(1)

    @pl.when(t == 0)
    def _():
        state_ref[...] = jnp.zeros(state_ref.shape, f32)

    ri = lax.broadcasted_iota(i32, (CHUNK, CHUNK), 0)
    ci = lax.broadcasted_iota(i32, (CHUNK, CHUNK), 1)
    tril = ri >= ci
    stril = ri > ci
    n_chunks = rows // CHUNK
    heads = range(DN_HEADS)
    pairs = [(h, c) for c in range(n_chunks) for h in heads]
    rows_of = lambda c: slice(c * CHUNK, (c + 1) * CHUNK)

    gcum_c, decay, kb = {}, {}, {}
    for p in pairs:
        h, c = p
        rs = rows_of(c)
        neg_a = -jnp.exp(jnp.zeros((1, 1), f32) + alog_ref[h])
        dtb = dtb_ref[h]
        g_col = neg_a * _softplus(small_ref[0, rs, SM_A + h:SM_A + h + 1] + dtb)
        g_row = neg_a * _softplus(ar_ref[0, h, :, rs] + dtb)
        gcum_c[p] = jnp.sum(jnp.where(tril, g_row, 0.0), axis=1, keepdims=True)
        gcum_r = jnp.sum(jnp.where(ri <= ci, g_col, 0.0), axis=0, keepdims=True)
        decay[p] = jnp.exp(jnp.where(tril, gcum_c[p] - gcum_r, -jnp.inf))
        kb[p] = k_ref[0, h, rs, :] * _sigmoid(small_ref[0, rs, SM_B + h:SM_B + h + 1])
    a = {p: jnp.where(stril, _dot(kb[p], k_ref[0, p[0], rows_of(p[1]), :], _NT) * decay[p], 0.0)
         for p in pairs}
    qk = {p: jnp.where(tril, _dot(q_ref[0, p[0], rows_of(p[1]), :],
                                  k_ref[0, p[0], rows_of(p[1]), :], _NT) * decay[p], 0.0)
          for p in pairs}
    r = {p: -a[p] for p in pairs}
    pw = a
    for _ in range(CHUNK.bit_length() - 2):
        pw = {p: _dot(pw[p], pw[p]) for p in pairs}
        r = {p: r[p] + pw[p] + _dot(r[p], pw[p]) for p in pairs}
    sol = {}
    for p in pairs:
        h, c = p
        rs = rows_of(c)
        beta = _sigmoid(small_ref[0, rs, SM_B + h:SM_B + h + 1])
        rhs = jnp.concatenate([v_ref[0, h, rs, :] * beta, kb[p] * jnp.exp(gcum_c[p])], axis=1)
        sol[p] = rhs + _dot(r[p], rhs)

    for c in range(n_chunks):
        rs = rows_of(c)
        state = {h: state_ref[h] for h in heads}
        v_new = {h: sol[(h, c)][:, :DN_DV] - _dot(sol[(h, c)][:, DN_DV:], state[h]) for h in heads}
        o_state = {h: _dot(q_ref[0, h, rs, :] * jnp.exp(gcum_c[(h, c)]), state[h]) for h in heads}
        o_local = {h: _dot(qk[(h, c)], v_new[h]) for h in heads}
        for h in heads:
            g = gcum_c[(h, c)]
            g_last = g[CHUNK - 1:CHUNK, :]
            k_tail = k_ref[0, h, rs, :] * jnp.exp(g_last - g)
            state_ref[h] = state[h] * jnp.exp(g_last) + _dot(k_tail, v_new[h], _TN)
        for h in heads:
            hc = slice(h * DN_DV, (h + 1) * DN_DV)
            o = o_state[h] + o_local[h]
            on = o * lax.rsqrt(jnp.mean(o * o, axis=1, keepdims=True) + RMS_EPS) * ng_ref[...]
            o_ref[0, rs, hc] = (on * _silu(z_ref[0, rs, hc])).astype(o_ref.dtype)


def _dn_chunk(q, k, v, small, a_row, z, a_log, dt_bias, norm_g, *, rows=128):
    bsz, heads, seq, dk = q.shape
    rows = min(rows, seq)
    hs = pl.BlockSpec((1, heads, rows, dk), lambda b, t, *_: (b, 0, t, 0))
    sms = pl.BlockSpec((1, rows, LANES), lambda b, t, *_: (b, t, 0))
    rsp = pl.BlockSpec((1, heads, 1, rows), lambda b, t, *_: (b, 0, 0, t))
    zs = pl.BlockSpec((1, rows, heads * DN_DV), lambda b, t, *_: (b, t, 0))
    return pl.pallas_call(
        functools.partial(_dnchunk_kernel, rows=rows),
        grid_spec=pltpu.PrefetchScalarGridSpec(
            num_scalar_prefetch=2,
            grid=(bsz, seq // rows),
            in_specs=[hs, hs, hs, sms, rsp, zs,
                      pl.BlockSpec((1, DN_DV), lambda b, t, *_: (0, 0))],
            out_specs=zs,
            scratch_shapes=[pltpu.VMEM((heads, DN_DK, DN_DV), f32)]),
        out_shape=jax.ShapeDtypeStruct((bsz, seq, heads * DN_DV), bf16),
        compiler_params=_cparams(("parallel", "arbitrary")),
        name="dn_chunk",
    )(a_log, dt_bias, q, k, v, small, a_row, z, norm_g.reshape(1, DN_DV))


def _layer_norm(r, g, b):
    mu = jnp.mean(r, axis=-1, keepdims=True)
    d = r - mu
    var = jnp.mean(d * d, axis=-1, keepdims=True)
    return d * lax.rsqrt(var + LN_EPS) * g + b


def _lane_min_index(hit, lane):
    return jnp.min(jnp.where(hit, lane, float(LANES)), axis=1, keepdims=True)


def _merge_kernel(ya_ref, yd_ref, ga_ref, gd_ref, x_ref, gt_ref, lng_ref, lnb_ref, sc_ref, sh_ref,
                  wa_ref, wd_ref, wo_ref, wr_ref, x1_ref, h2_ref, eidx_ref, gate_ref, *, alpha):
    merged = (ga_ref[0] * _dot(ya_ref[0], wa_ref[...]) + gd_ref[0] * _dot(yd_ref[0], wd_ref[...]))
    y = _dot(merged, wo_ref[...])
    x1 = _layer_norm(alpha * x_ref[0] + (1.0 + gt_ref[0]) * y, lng_ref[...], lnb_ref[...])
    x1_ref[0] = x1
    h2 = x1 * (1.0 + sc_ref[0]) + sh_ref[0]
    h2_ref[0] = h2
    logits = _dot(h2, wr_ref[...], exact=True)
    lane = lax.broadcasted_iota(i32, logits.shape, 1).astype(f32)
    lg = jnp.where(lane < N_GROUPS, logits, -jnp.inf)
    mg = jnp.max(lg, axis=1, keepdims=True)
    top_gp = 1.0 / jnp.sum(jnp.exp(lg - mg), axis=1, keepdims=True)
    g_idx = _lane_min_index(lg == mg, lane)
    lo = N_GROUPS + g_idx * EXPERTS_PER_GROUP
    in_grp = jnp.logical_and(lane >= lo, lane < lo + EXPERTS_PER_GROUP)
    le = jnp.where(in_grp, logits, -jnp.inf)
    m1 = jnp.max(le, axis=1, keepdims=True)
    i1 = _lane_min_index(le == m1, lane)
    le2 = jnp.where(lane == i1, -jnp.inf, le)
    m2 = jnp.max(le2, axis=1, keepdims=True)
    i2 = _lane_min_index(le2 == m2, lane)
    e2 = jnp.exp(m2 - m1)
    gate1 = top_gp / (1.0 + e2)
    gate2 = top_gp * e2 / (1.0 + e2)
    e_lanes = jnp.where(lane == 0.0, i1 - N_GROUPS, jnp.where(lane == 1.0, i2 - N_GROUPS, 0.0))
    eidx_ref[0] = e_lanes.astype(i32)
    gate_ref[0] = jnp.where(lane == 0.0, gate1, jnp.where(lane == 1.0, gate2, 0.0))


def _merge(y_att, y_dn, gates, x, gt1, ln_g, ln_b, sc2, sh2, w_br_att, w_br_dn, w_out, w_route,
           alpha, *, tm=256):
    bsz, seq, d = x.shape
    tm = min(tm, seq)
    row = lambda b, i: (b, i, 0)
    per_b = pl.BlockSpec((1, 1, d), lambda b, i: (b, 0, 0))
    vec = pl.BlockSpec((1, d), lambda b, i: (0, 0))
    wsp = pl.BlockSpec((d, d), lambda b, i: (0, 0))
    out_f = jax.ShapeDtypeStruct((bsz, seq, d), f32)
    return pl.pallas_call(
        functools.partial(_merge_kernel, alpha=alpha),
        grid=(bsz, seq // tm),
        in_specs=[pl.BlockSpec((1, tm, d), row), pl.BlockSpec((1, tm, d), row),
                  pl.BlockSpec((1, tm, d), lambda b, i: (b, i, 0)),
                  pl.BlockSpec((1, tm, d), lambda b, i: (b, i, 1)),
                  pl.BlockSpec((1, tm, d), row), per_b, vec, vec, per_b, per_b,
                  wsp, wsp, wsp, pl.BlockSpec((d, LANES), lambda b, i: (0, 0))],
        out_specs=[pl.BlockSpec((1, tm, d), row), pl.BlockSpec((1, tm, d), row),
                   pl.BlockSpec((1, tm, LANES), row), pl.BlockSpec((1, tm, LANES), row)],
        out_shape=[out_f, out_f, jax.ShapeDtypeStruct((bsz, seq, LANES), i32),
                   jax.ShapeDtypeStruct((bsz, seq, LANES), f32)],
        compiler_params=_cparams(("parallel", "parallel")),
        name="merge_router",
    )(y_att, y_dn, gates, gates, x, gt1, ln_g, ln_b, sc2, sh2, w_br_att, w_br_dn, w_out, w_route)


def _expert_kernel(blk_e_ref, tok_ref, slot_ref, h_hbm, wg_ref, wu_ref, wd_ref, y_hbm,
                   xbuf, ybuf, gsem, ssem):
    i = pl.program_id(0)
    nb = pl.num_programs(0)
    cur = i % 2

    def gather_copy(blk, buf, r):
        tok = tok_ref[blk * MOE_BLOCK + r]
        return pltpu.make_async_copy(h_hbm.at[pl.ds(tok, 1), :], xbuf.at[buf, pl.ds(r, 1), :],
                                     gsem.at[buf])

    def scatter_copy(blk, buf, r):
        slot = slot_ref[blk * MOE_BLOCK + r]
        return pltpu.make_async_copy(ybuf.at[buf, pl.ds(r, 1), :], y_hbm.at[pl.ds(slot, 1), :],
                                     ssem.at[buf])

    def start_rows(copy_fn, blk, buf):
        def body(r, carry):
            copy_fn(blk, buf, r).start()
            return carry
        lax.fori_loop(0, MOE_BLOCK, body, 0, unroll=8)

    def wait_gather(buf):
        pltpu.make_async_copy(h_hbm.at[pl.ds(0, MOE_BLOCK), :], xbuf.at[buf], gsem.at[buf]).wait()

    def wait_scatter(buf):
        pltpu.make_async_copy(ybuf.at[buf], y_hbm.at[pl.ds(0, MOE_BLOCK), :], ssem.at[buf]).wait()

    def start_rows_inline(copy_fn, blk, buf):
        for r in range(MOE_BLOCK):
            copy_fn(blk, buf, r).start()

    def ffn(buf):
        xb = xbuf[buf]
        hb = _silu(_dot(xb, wg_ref[0])) * _dot(xb, wu_ref[0])
        return _dot(hb, wd_ref[0])

    gcur = i % GATHER_SLOTS
    gnew = (i + 2) % GATHER_SLOTS

    @pl.when(i == 0)
    def _():
        start_rows(gather_copy, 0, 0)
        start_rows(gather_copy, 1, 1)

    wait_gather(gcur)

    @pl.when(i >= 2)
    def _():
        wait_scatter(cur)

    @pl.when(i == 0)
    def _():
        start_rows_inline(gather_copy, i + 2, gnew)
        ybuf[cur] = ffn(gcur)

    @pl.when(jnp.logical_and(i > 0, i < nb - 2))
    def _():
        start_rows_inline(gather_copy, i + 2, gnew)
        start_rows_inline(scatter_copy, i - 1, 1 - cur)
        ybuf[cur] = ffn(gcur)

    @pl.when(i >= nb - 2)
    def _():
        start_rows_inline(scatter_copy, i - 1, 1 - cur)
        ybuf[cur] = ffn(gcur)

    @pl.when(i == nb - 1)
    def _():
        start_rows(scatter_copy, i, cur)
        wait_scatter(1 - cur)
        wait_scatter(cur)


def _experts(h2, blk_e, tok_buf, slot_buf, w_gate, w_up, w_down, n_slots):
    n_tok, d = h2.shape
    nb = blk_e.shape[0]
    ff = w_gate.shape[-1]
    assert nb >= 4
    return pl.pallas_call(
        _expert_kernel,
        grid_spec=pltpu.PrefetchScalarGridSpec(
            num_scalar_prefetch=3,
            grid=(nb,),
            in_specs=[pl.BlockSpec(memory_space=pl.ANY),
                      pl.BlockSpec((1, d, ff), lambda i, be, *_: (be[i], 0, 0)),
                      pl.BlockSpec((1, d, ff), lambda i, be, *_: (be[i], 0, 0)),
                      pl.BlockSpec((1, ff, d), lambda i, be, *_: (be[i], 0, 0))],
            out_specs=pl.BlockSpec(memory_space=pl.ANY),
            scratch_shapes=[pltpu.VMEM((GATHER_SLOTS, MOE_BLOCK, d), f32),
                            pltpu.VMEM((2, MOE_BLOCK, d), f32),
                            pltpu.SemaphoreType.DMA((GATHER_SLOTS,)),
                            pltpu.SemaphoreType.DMA((2,))]),
        out_shape=jax.ShapeDtypeStruct((n_slots, d), f32),
        compiler_params=_cparams(("arbitrary",)),
        name="moe_experts",
    )(blk_e, tok_buf, slot_buf, h2, w_gate, w_up, w_down)


def _route_plan(e_idx, n_tok):
    m = n_tok * EXPERT_TOPK
    flat_e = e_idx.reshape(m)
    order = jnp.argsort(flat_e, stable=True).astype(i32)
    experts = jnp.arange(N_EXPERTS, dtype=i32)
    counts = jnp.sum((flat_e[:, None] == experts[None, :]).astype(i32), axis=0)
    pcounts = (counts + MOE_BLOCK - 1) // MOE_BLOCK * MOE_BLOCK
    starts = jnp.cumsum(counts) - counts
    pends = jnp.cumsum(pcounts)
    pstarts = pends - pcounts
    nb = -(-m // MOE_BLOCK) + N_EXPERTS
    rows = nb * MOE_BLOCK
    blk_raw = jnp.sum((pends[None, :] <= (jnp.arange(nb, dtype=i32) * MOE_BLOCK)[:, None]).astype(i32),
                      axis=1)
    blk_e = jnp.minimum(blk_raw, N_EXPERTS - 1)
    pos = jnp.arange(rows, dtype=i32)
    in_region = jnp.repeat(blk_raw, MOE_BLOCK) < N_EXPERTS
    e_row = jnp.repeat(blk_e, MOE_BLOCK)
    rank = pos - pstarts[e_row]
    valid = jnp.logical_and(in_region, rank < counts[e_row])
    assign = order[jnp.clip(starts[e_row] + rank, 0, m - 1)]
    n_before = jnp.where(in_region, starts[e_row] + jnp.minimum(rank, counts[e_row]), m)
    slot = (assign % EXPERT_TOPK) * n_tok + assign // EXPERT_TOPK
    slot_buf = jnp.where(valid, slot, m + pos - n_before)
    tok_buf = jnp.where(valid, assign // EXPERT_TOPK, 0)
    return blk_e, tok_buf, slot_buf, rows - m


def _final_kernel(y0_ref, y1_ref, g_ref, x_ref, gt_ref, lng_ref, lnb_ref, o_ref, *, alpha):
    y = y0_ref[...] * g_ref[0, :, 0:1] + y1_ref[...] * g_ref[0, :, 1:2]
    o_ref[0] = _layer_norm(alpha * x_ref[0] + (1.0 + gt_ref[0]) * y, lng_ref[...], lnb_ref[...])


def _final(y_slots, g_lanes, x, gt2, ln_g, ln_b, alpha, *, tm=512):
    bsz, seq, d = x.shape
    tm = min(tm, seq)
    nblk = seq // tm
    second = bsz * nblk
    per_b = pl.BlockSpec((1, 1, d), lambda b, i: (b, 0, 0))
    vec = pl.BlockSpec((1, d), lambda b, i: (0, 0))
    return pl.pallas_call(
        functools.partial(_final_kernel, alpha=alpha),
        grid=(bsz, nblk),
        in_specs=[pl.BlockSpec((tm, d), lambda b, i: (b * nblk + i, 0)),
                  pl.BlockSpec((tm, d), lambda b, i: (second + b * nblk + i, 0)),
                  pl.BlockSpec((1, tm, LANES), lambda b, i: (b, i, 0)),
                  pl.BlockSpec((1, tm, d), lambda b, i: (b, i, 0)), per_b, vec, vec],
        out_specs=pl.BlockSpec((1, tm, d), lambda b, i: (b, i, 0)),
        out_shape=jax.ShapeDtypeStruct((bsz, seq, d), f32),
        compiler_params=_cparams(("parallel", "parallel")),
        name="moe_combine_ln",
    )(y_slots, y_slots, g_lanes, x, gt2, ln_g, ln_b)


def _pack_w_in(w_in):
    sizes = (ATT_HEADS * ATT_HEAD_DIM, KV_LATENT, IDX_HEADS * IDX_DIM, IDX_DIM, IDX_HEADS,
             3 * DN_HEADS * DN_DK, DN_HEADS, DN_HEADS, DN_HEADS * DN_DV, 2 * w_in.shape[0])
    offs = [0]
    for s in sizes:
        offs.append(offs[-1] + s)
    w_in = w_in.astype(bf16)
    seg = [w_in[:, offs[j]:offs[j + 1]] for j in range(len(sizes))]
    w_q, w_ckv, w_qidx, w_kidx, w_widx, w_qkv, w_a, w_b, w_z, w_gates = seg
    pad = jnp.zeros((w_in.shape[0], LANES - (IDX_DIM + IDX_HEADS + 2 * DN_HEADS)), w_in.dtype)
    packed = jnp.concatenate([w_q, w_ckv, w_qidx, w_kidx, w_widx, w_a, w_b, pad, w_qkv, w_z, w_gates],
                             axis=1)
    return packed


def kernel(x, c, w_ada, b_ada, w_in, kv_norm_g, w_uk, w_uv, conv_w, a_log, dt_bias, dn_norm_g,
           w_br_att, w_br_dn, w_out, w_route_grp, w_route_exp, w_gate, w_up, w_down, ln_g, ln_b):
    depth = w_in.shape[0]
    bsz, seq, d = x.shape
    n_tok = bsz * seq
    alpha = (2.0 * depth) ** 0.25
    mod = _ada(c, w_ada, b_ada)
    for l in range(depth):
        sh1, sc1, gt1, sh2, sc2, gt2 = [mod[l, :, j * d:(j + 1) * d].reshape(bsz, 1, d)
                                        for j in range(6)]
        q_lat, kv, q_idx, small, qkv, z, gates, k_idx = _proj(
            x, sc1, sh1, _pack_w_in(w_in[l]), jnp.swapaxes(w_uk[l], 1, 2).astype(bf16), kv_norm_g[l])

        y_att = _dsa_attention(q_lat, q_idx, small, k_idx, kv, w_uv[l].astype(bf16))

        dq, dk, dv = _dn_prep(qkv, conv_w[l])
        a_row = jnp.swapaxes(small[..., SM_A:SM_A + DN_HEADS], 1, 2)[:, :, None, :]
        y_dn = _dn_chunk(dq, dk, dv, small, a_row, z, a_log[l], dt_bias[l], dn_norm_g[l])

        w_route = jnp.zeros((d, LANES), f32)
        w_route = w_route.at[:, :N_GROUPS].set(w_route_grp[l])
        w_route = w_route.at[:, N_GROUPS:N_GROUPS + N_EXPERTS].set(w_route_exp[l])
        x1, h2, e_lanes, g_lanes = _merge(
            y_att, y_dn, gates, x, gt1, ln_g[l, 0].reshape(1, d), ln_b[l, 0].reshape(1, d),
            sc2, sh2, w_br_att[l].astype(bf16), w_br_dn[l].astype(bf16), w_out[l].astype(bf16),
            w_route, alpha)

        e_idx = e_lanes.reshape(n_tok, LANES)[:, :EXPERT_TOPK]
        blk_e, tok_buf, slot_buf, n_pad = _route_plan(e_idx, n_tok)
        y_slots = _experts(h2.reshape(n_tok, d), blk_e, tok_buf, slot_buf,
                           w_gate[l], w_up[l], w_down[l],
                           n_tok * EXPERT_TOPK + n_pad)
        x = _final(y_slots, g_lanes, x1, gt2, ln_g[l, 1].reshape(1, d), ln_b[l, 1].reshape(1, d), alpha)
    return x
```

```python
import functools

import jax
import jax.numpy as jnp
from jax import lax
from jax.experimental import pallas as pl
from jax.experimental.pallas import tpu as pltpu

f32 = jnp.float32
bf16 = jnp.bfloat16
i32 = jnp.int32

ATT_HEADS = 8
ATT_HEAD_DIM = 128
KV_LATENT = 256
IDX_HEADS = 8
IDX_DIM = 64
TOPK_MAX = 256
DN_HEADS = 8
DN_DK = 128
DN_DV = 128
CONV_WIDTH = 4
CHUNK = 64
N_GROUPS = 4
EXPERTS_PER_GROUP = 8
N_EXPERTS = N_GROUPS * EXPERTS_PER_GROUP
EXPERT_TOPK = 2
EXPERT_FF = 512
MOE_BLOCK = 128
GATHER_SLOTS = 3
LN_EPS = 1e-5
RMS_EPS = 1e-6
ATT_SCALE = ATT_HEAD_DIM ** -0.5
INDEX_SCALE = (IDX_HEADS ** -0.5) * (IDX_DIM ** -0.5)
LOG2_E = 1.4426950408889634

LANES = 128
SUBLANES = 8
VMEM_LIMIT = 56 * 1024 * 1024
INT_MIN = -(2 ** 31)
NEG_BIG = -1e30

SM_KIDX = 0
SM_WIDX = IDX_DIM
SM_A = SM_WIDX + IDX_HEADS
SM_B = SM_A + DN_HEADS

_NT = (((1,), (1,)), ((), ()))
_TN = (((0,), (0,)), ((), ()))


def _dot(a, b, dims=None, exact=False):
    if dims is None:
        dims = (((a.ndim - 1,), (0,)), ((), ()))
    if exact:
        return lax.dot_general(a.astype(f32), b.astype(f32), dims,
                               precision=lax.Precision.HIGHEST, preferred_element_type=f32)
    return lax.dot_general(a.astype(bf16), b.astype(bf16), dims, preferred_element_type=f32)


def _sigmoid(x):
    return 1.0 / (1.0 + jnp.exp(-x))


def _silu(x):
    return x * _sigmoid(x)


def _cparams(sem):
    return pltpu.CompilerParams(dimension_semantics=sem, vmem_limit_bytes=VMEM_LIMIT)


def _ada_kernel(c_ref, w_ref, b_ref, o_ref):
    cond = _silu(c_ref[...])
    o_ref[0] = _dot(cond, w_ref[0], exact=True) + b_ref[0]


def _ada(c, w_ada, b_ada):
    depth, d, n = w_ada.shape
    b = c.shape[0]
    rows = max(SUBLANES, -(-b // SUBLANES) * SUBLANES)
    cp = jnp.zeros((rows, d), f32).at[:b].set(c)
    tn = 1536
    out = pl.pallas_call(
        _ada_kernel,
        grid=(depth, n // tn),
        in_specs=[pl.BlockSpec((rows, d), lambda l, j: (0, 0)),
                  pl.BlockSpec((1, d, tn), lambda l, j: (l, 0, j)),
                  pl.BlockSpec((1, 1, tn), lambda l, j: (l, 0, j))],
        out_specs=pl.BlockSpec((1, rows, tn), lambda l, j: (l, 0, j)),
        out_shape=jax.ShapeDtypeStruct((depth, rows, n), f32),
        compiler_params=_cparams(("parallel", "parallel")),
        name="ada_mod",
    )(cp, w_ada, b_ada.reshape(depth, 1, n))
    return out[:, :b]


PW_Q = 0
PW_CKV = PW_Q + ATT_HEADS * ATT_HEAD_DIM
PW_QIDX = PW_CKV + KV_LATENT
PW_SMALL = PW_QIDX + IDX_HEADS * IDX_DIM
PW_QKV = PW_SMALL + LANES
PW_Z = PW_QKV + 3 * DN_HEADS * DN_DK
PW_GATES = PW_Z + DN_HEADS * DN_DV
PROJ_TN = 1024


def _proj_kernel(x_ref, sc_ref, sh_ref, w_ref, wuk_ref, g_ref,
                 qlat_ref, kv_ref, qidx_ref, small_ref, qkv_ref, z_ref, gates_ref, kidx_ref):
    h = (x_ref[0] * (1.0 + sc_ref[0]) + sh_ref[0]).astype(bf16)

    def cols(start, width):
        return _dot(h, w_ref[:, start:start + width])

    q_att = cols(PW_Q, ATT_HEADS * ATT_HEAD_DIM)
    for j in range(ATT_HEADS):
        ql = _dot(q_att[:, j * ATT_HEAD_DIM:(j + 1) * ATT_HEAD_DIM], wuk_ref[j]) * (ATT_SCALE * LOG2_E)
        qlat_ref[0, :, j * KV_LATENT:(j + 1) * KV_LATENT] = ql.astype(qlat_ref.dtype)
    c_kv = cols(PW_CKV, KV_LATENT)
    c_kv = c_kv * lax.rsqrt(jnp.mean(c_kv * c_kv, axis=-1, keepdims=True) + RMS_EPS)
    kv_ref[0] = (c_kv * g_ref[...]).astype(kv_ref.dtype)
    qidx_ref[0] = cols(PW_QIDX, IDX_HEADS * IDX_DIM).astype(qidx_ref.dtype)
    small = cols(PW_SMALL, LANES)
    small_ref[0] = small
    kidx_ref[0] = small[:, SM_KIDX:SM_KIDX + IDX_DIM].astype(kidx_ref.dtype)
    for c in range(qkv_ref.shape[-1] // PROJ_TN):
        qkv_ref[0, :, c * PROJ_TN:(c + 1) * PROJ_TN] = cols(PW_QKV + c * PROJ_TN, PROJ_TN)
    z_ref[0] = cols(PW_Z, DN_HEADS * DN_DV)
    for c in range(gates_ref.shape[-1] // PROJ_TN):
        gates_ref[0, :, c * PROJ_TN:(c + 1) * PROJ_TN] = _sigmoid(cols(PW_GATES + c * PROJ_TN, PROJ_TN))


def _proj(x, sc, sh, w_packed, w_ukt, kv_norm_g, *, tm=256):
    bsz, seq, d = x.shape
    tm = min(tm, seq)
    n = w_packed.shape[1]
    widths = (ATT_HEADS * KV_LATENT, KV_LATENT, IDX_HEADS * IDX_DIM, LANES, 3 * DN_HEADS * DN_DK,
              DN_HEADS * DN_DV, 2 * d, IDX_DIM)
    dtypes = (bf16, bf16, bf16, f32, f32, f32, f32, bf16)
    row = lambda b, i: (b, i, 0)
    per_b = pl.BlockSpec((1, 1, d), lambda b, i: (b, 0, 0))
    once = pl.Buffered(1)
    return pl.pallas_call(
        _proj_kernel,
        grid=(bsz, seq // tm),
        in_specs=[pl.BlockSpec((1, tm, d), row), per_b, per_b,
                  pl.BlockSpec((d, n), lambda b, i: (0, 0), pipeline_mode=once),
                  pl.BlockSpec(w_ukt.shape, lambda b, i: (0, 0, 0), pipeline_mode=once),
                  pl.BlockSpec((1, KV_LATENT), lambda b, i: (0, 0))],
        out_specs=[pl.BlockSpec((1, tm, wd), row) for wd in widths],
        out_shape=[jax.ShapeDtypeStruct((bsz, seq, wd), dt) for wd, dt in zip(widths, dtypes)],
        compiler_params=_cparams(("parallel", "parallel")),
        name="proj_in",
    )(x, sc, sh, w_packed, w_ukt, kv_norm_g.reshape(1, KV_LATENT))


def _attn_kernel(qlat_ref, qidx_ref, small_ref, kidx_ref, kv_ref, wuv_ref, o_ref,
                 keys_ref, qs_ref, qis_ref, s_ref, bias_ref, p_ref, acc_ref, m_ref, l_ref, al_ref,
                 *, tq, tk, topk, seq):
    i = pl.program_id(1)
    q0 = i * tq
    nkb = (q0 + tq + tk - 1) // tk
    nkb2 = nkb + (nkb & 1)
    int_min = jnp.int32(INT_MIN)
    row = q0 + lax.broadcasted_iota(i32, (tq, tk), 0)
    col = lax.broadcasted_iota(i32, (tq, tk), 1)
    lane = lax.broadcasted_iota(i32, (tq, LANES), 1)
    nl = tk // LANES

    for h in range(IDX_HEADS):
        qis_ref[h * tq:(h + 1) * tq, :] = qidx_ref[0, :, h * IDX_DIM:(h + 1) * IDX_DIM]
    wts = small_ref[0][:, SM_WIDX:SM_WIDX + IDX_HEADS] * INDEX_SCALE

    def idx_scores(kb):
        kblk = kidx_ref[0, pl.ds(pl.multiple_of(kb * tk, tk), tk), :]
        return _dot(qis_ref[...], kblk, _NT)

    def score_half(kb, cur, nxt):
        s_ref[nxt] = idx_scores(jnp.minimum(kb + 1, nkb2 - 1))
        c0 = pl.multiple_of(kb * tk, tk)
        acc = jnp.zeros((tq, tk), f32)
        for h in range(IDX_HEADS):
            acc = acc + jnp.maximum(s_ref[cur, h * tq:(h + 1) * tq, :], 0.0) * wts[:, h:h + 1]
        bits = pltpu.bitcast(acc, i32)
        key = bits ^ ((bits >> 31) & jnp.int32(0x7FFFFFFF))
        key = jnp.where(c0 + col <= row, key, int_min)
        keys_ref[:, pl.ds(c0, tk)] = key

    def score_body(j, carry):
        score_half(2 * j, 0, 1)
        score_half(2 * j + 1, 1, 0)
        return carry

    s_ref[0] = idx_scores(0)
    lax.fori_loop(0, nkb2 // 2, score_body, 0)

    def count(pred):
        def body(kb2, part):
            c0 = pl.multiple_of(kb2 * (2 * tk), 2 * tk)
            blk = keys_ref[:, pl.ds(c0, 2 * tk)]
            for j in range(2 * nl):
                part = part + pred(blk[:, j * LANES:(j + 1) * LANES], c0 + j * LANES + lane)
            return part
        part = lax.fori_loop(0, nkb2 // 2, body, jnp.zeros((tq, LANES), f32))
        return jnp.sum(part, axis=1, keepdims=True)

    kf = jnp.float32(topk)

    def radix_body(it, carry):
        thr, cnt_thr = carry
        cand = thr + (jnp.int32(1) << (31 - it))
        cand_b = jnp.broadcast_to(cand, (tq, LANES))
        cnt = count(lambda k, c: jnp.where(k >= cand_b, 1.0, 0.0))
        ok = cnt >= kf
        return jnp.where(ok, cand, thr), jnp.where(ok, cnt, cnt_thr)

    thr0 = jnp.full((tq, 1), INT_MIN, i32)
    cnt0 = jnp.zeros((tq, 1), f32) + (nkb2 * tk).astype(f32)
    thr, cnt_thr = lax.fori_loop(0, 32, radix_body, (thr0, cnt0))

    need = jnp.logical_and(cnt_thr > kf, thr > int_min)
    any_need = jnp.max(jnp.where(need, 1.0, 0.0)) > 0.0

    @pl.when(any_need)
    def _():
        thr_b = jnp.broadcast_to(thr, (tq, LANES))
        n_gt = count(lambda k, c: jnp.where(k > thr_b, 1.0, 0.0))
        quota = kf - n_gt

        def cut_body(it, cut):
            cand = cut + (jnp.int32(1) << (seq.bit_length() - 1 - it))
            cand_b = jnp.broadcast_to(cand, (tq, LANES))
            cnt = count(lambda k, c: jnp.where(k == thr_b, jnp.where(c < cand_b, 1.0, 0.0), 0.0))
            return jnp.where(cnt <= quota, cand, cut)

        cut = lax.fori_loop(0, seq.bit_length(), cut_body, jnp.zeros((tq, 1), i32))

        def drop_body(kb, carry):
            c0 = pl.multiple_of(kb * tk, tk)
            blk = keys_ref[:, pl.ds(c0, tk)]
            dropped = jnp.where(c0 + col >= cut, int_min, blk)
            keys_ref[:, pl.ds(c0, tk)] = jnp.where(blk == thr, dropped, blk)
            return carry

        lax.fori_loop(0, nkb, drop_body, 0)

    thr_eff = jnp.maximum(thr, int_min + 1)
    for h in range(ATT_HEADS):
        qs_ref[h * tq:(h + 1) * tq, :] = qlat_ref[0, :, h * KV_LATENT:(h + 1) * KV_LATENT]
    m_ref[...] = jnp.full(m_ref.shape, NEG_BIG, f32)
    l_ref[...] = jnp.zeros(l_ref.shape, f32)
    acc_ref[...] = jnp.zeros(acc_ref.shape, f32)

    p_ref[1] = jnp.zeros(p_ref.shape[1:], bf16)
    al_ref[1] = jnp.ones(al_ref.shape[1:], f32)

    def kv_block(kb):
        return kv_ref[0, pl.ds(pl.multiple_of(kb * tk, tk), tk), :]

    def accumulate(kb, slot):
        pv = _dot(p_ref[slot], kv_block(kb))
        for j in range(KV_LATENT // LANES):
            js = slice(j * LANES, (j + 1) * LANES)
            acc_ref[:, js] = al_ref[slot] * acc_ref[:, js] + pv[:, js]

    def half_step(kb, cur, nxt):
        s_ref[nxt] = _dot(qs_ref[...], kv_block(jnp.minimum(kb + 1, nkb2 - 1)), _NT)
        accumulate(jnp.maximum(kb - 1, 0), nxt)
        c0 = pl.multiple_of(kb * tk, tk)
        bias_ref[...] = jnp.where(keys_ref[:, pl.ds(c0, tk)] >= thr_eff, 0.0, NEG_BIG)
        for h in range(ATT_HEADS):
            hs = slice(h * tq, (h + 1) * tq)
            mx = None
            for j in range(nl):
                js = slice(j * LANES, (j + 1) * LANES)
                t = s_ref[cur, hs, js] + bias_ref[:, js]
                s_ref[cur, hs, js] = t
                mx = t if mx is None else jnp.maximum(mx, t)
            m_prev = m_ref[hs, :]
            m_new = jnp.maximum(m_prev, jnp.max(mx, axis=1, keepdims=True))
            m_ref[hs, :] = m_new
            al_ref[cur, hs, :] = jnp.exp2(m_prev - m_new)
        for h in range(ATT_HEADS):
            hs = slice(h * tq, (h + 1) * tq)
            m_new = m_ref[hs, :]
            ps = None
            for j in range(nl):
                js = slice(j * LANES, (j + 1) * LANES)
                p = jnp.exp2(s_ref[cur, hs, js] - m_new)
                p_ref[cur, hs, js] = p.astype(bf16)
                ps = p if ps is None else ps + p
            l_ref[hs, :] = al_ref[cur, hs, :] * l_ref[hs, :] + ps

    def att_body(j, carry):
        half_step(2 * j, 0, 1)
        half_step(2 * j + 1, 1, 0)
        return carry

    s_ref[0] = _dot(qs_ref[...], kv_block(0), _NT)
    lax.fori_loop(0, nkb2 // 2, att_body, 0)
    accumulate(nkb2 - 1, 1)

    inv_l = 1.0 / jnp.sum(l_ref[...], axis=1, keepdims=True)
    o_lat = acc_ref[...] * inv_l
    for h in range(ATT_HEADS):
        y = _dot(o_lat[h * tq:(h + 1) * tq, :], wuv_ref[h])
        o_ref[0, :, h * ATT_HEAD_DIM:(h + 1) * ATT_HEAD_DIM] = y.astype(o_ref.dtype)


def _dsa_attention(q_lat, q_idx, small, k_idx, kv, w_uv, *, tq=128, tk=512):
    bsz, seq, _ = q_lat.shape
    topk = min(TOPK_MAX, seq // 4)
    tk = min(tk, seq)
    assert tk >= topk and seq % (2 * tk) == 0 and seq % tq == 0
    rows = ATT_HEADS * tq
    return pl.pallas_call(
        functools.partial(_attn_kernel, tq=tq, tk=tk, topk=topk, seq=seq),
        grid=(bsz, seq // tq),
        in_specs=[pl.BlockSpec((1, tq, ATT_HEADS * KV_LATENT), lambda b, i: (b, i, 0)),
                  pl.BlockSpec((1, tq, IDX_HEADS * IDX_DIM), lambda b, i: (b, i, 0)),
                  pl.BlockSpec((1, tq, LANES), lambda b, i: (b, i, 0)),
                  pl.BlockSpec((1, seq, IDX_DIM), lambda b, i: (b, 0, 0)),
                  pl.BlockSpec((1, seq, KV_LATENT), lambda b, i: (b, 0, 0)),
                  pl.BlockSpec((ATT_HEADS, KV_LATENT, ATT_HEAD_DIM), lambda b, i: (0, 0, 0))],
        out_specs=pl.BlockSpec((1, tq, ATT_HEADS * ATT_HEAD_DIM), lambda b, i: (b, i, 0)),
        out_shape=jax.ShapeDtypeStruct((bsz, seq, ATT_HEADS * ATT_HEAD_DIM), bf16),
        scratch_shapes=[pltpu.VMEM((tq, seq), i32),
                        pltpu.VMEM((rows, KV_LATENT), bf16),
                        pltpu.VMEM((rows, IDX_DIM), bf16),
                        pltpu.VMEM((2, rows, tk), f32),
                        pltpu.VMEM((tq, tk), f32),
                        pltpu.VMEM((2, rows, tk), bf16),
                        pltpu.VMEM((rows, KV_LATENT), f32),
                        pltpu.VMEM((rows, LANES), f32),
                        pltpu.VMEM((rows, LANES), f32),
                        pltpu.VMEM((2, rows, LANES), f32)],
        compiler_params=_cparams(("parallel", "arbitrary")),
        name="dsa_attention",
    )(q_lat, q_idx, small, k_idx, kv, w_uv)


def _dnprep_kernel(cur_ref, prev_ref, w_ref, q_ref, k_ref, v_ref, xx_ref, *, ts):
    i = pl.program_id(1)
    halo = SUBLANES
    xx_ref[0:halo, :] = jnp.where(i > 0, prev_ref[0], 0.0)
    xx_ref[halo:halo + ts, :] = cur_ref[0]
    for g in range(3 * DN_HEADS):
        cs = slice(g * LANES, (g + 1) * LANES)
        y = jnp.zeros((ts, LANES), f32)
        for j in range(CONV_WIDTH):
            off = halo - (CONV_WIDTH - 1) + j
            y = y + w_ref[j:j + 1, cs] * xx_ref[off:off + ts, cs]
        y = _silu(y)
        if g < 2 * DN_HEADS:
            y = y * lax.rsqrt(jnp.sum(y * y, axis=1, keepdims=True) + RMS_EPS)
        if g < DN_HEADS:
            q_ref[0, g] = y * (DN_DK ** -0.5)
        elif g < 2 * DN_HEADS:
            k_ref[0, g - DN_HEADS] = y
        else:
            v_ref[0, g - 2 * DN_HEADS] = y


def _dn_prep(qkv, conv_w, *, ts=256):
    bsz, seq, ch = qkv.shape
    ts = min(ts, seq)
    hb = ts // SUBLANES
    head_out = jax.ShapeDtypeStruct((bsz, DN_HEADS, seq, DN_DK), f32)
    head_spec = pl.BlockSpec((1, DN_HEADS, ts, DN_DK), lambda b, i: (b, 0, i, 0))
    return pl.pallas_call(
        functools.partial(_dnprep_kernel, ts=ts),
        grid=(bsz, seq // ts),
        in_specs=[pl.BlockSpec((1, ts, ch), lambda b, i: (b, i, 0)),
                  pl.BlockSpec((1, SUBLANES, ch), lambda b, i: (b, jnp.maximum(i * hb - 1, 0), 0)),
                  pl.BlockSpec((CONV_WIDTH, ch), lambda b, i: (0, 0))],
        out_specs=[head_spec, head_spec, head_spec],
        out_shape=[head_out, head_out, head_out],
        scratch_shapes=[pltpu.VMEM((ts + SUBLANES, ch), f32)],
        compiler_params=_cparams(("parallel", "parallel")),
        name="dn_prep",
    )(qkv, qkv, conv_w)


def _softplus(x):
    return jnp.maximum(x, 0.0) + jnp.log(1.0 + jnp.exp(-jnp.abs(x)))


def _dnchunk_kernel(alog_ref, dtb_ref, q_ref, k_ref, v_ref, small_ref, ar_ref,
                    z_ref, ng_ref, o_ref, state_ref, *, rows):
    t = pl.program_id(1)

    @pl.when(t == 0)
    def _():
        state_ref[...] = jnp.zeros(state_ref.shape, f32)

    ri = lax.broadcasted_iota(i32, (CHUNK, CHUNK), 0)
    ci = lax.broadcasted_iota(i32, (CHUNK, CHUNK), 1)
    tril = ri >= ci
    stril = ri > ci
    n_chunks = rows // CHUNK
    heads = range(DN_HEADS)
    pairs = [(h, c) for c in range(n_chunks) for h in heads]
    rows_of = lambda c: slice(c * CHUNK, (c + 1) * CHUNK)

    gcum_c, decay, kb = {}, {}, {}
    for p in pairs:
        h, c = p
        rs = rows_of(c)
        neg_a = -jnp.exp(jnp.zeros((1, 1), f32) + alog_ref[h])
        dtb = dtb_ref[h]
        g_col = neg_a * _softplus(small_ref[0, rs, SM_A + h:SM_A + h + 1] + dtb)
        g_row = neg_a * _softplus(ar_ref[0, h, :, rs] + dtb)
        gcum_c[p] = jnp.sum(jnp.where(tril, g_row, 0.0), axis=1, keepdims=True)
        gcum_r = jnp.sum(jnp.where(ri <= ci, g_col, 0.0), axis=0, keepdims=True)
        decay[p] = jnp.exp(jnp.where(tril, gcum_c[p] - gcum_r, -jnp.inf))
        kb[p] = k_ref[0, h, rs, :] * _sigmoid(small_ref[0, rs, SM_B + h:SM_B + h + 1])
    a = {p: jnp.where(stril, _dot(kb[p], k_ref[0, p[0], rows_of(p[1]), :], _NT) * decay[p], 0.0)
         for p in pairs}
    qk = {p: jnp.where(tril, _dot(q_ref[0, p[0], rows_of(p[1]), :],
                                  k_ref[0, p[0], rows_of(p[1]), :], _NT) * decay[p], 0.0)
          for p in pairs}
    r = {p: -a[p] for p in pairs}
    pw = a
    for _ in range(CHUNK.bit_length() - 2):
        pw = {p: _dot(pw[p], pw[p]) for p in pairs}
        r = {p: r[p] + pw[p] + _dot(r[p], pw[p]) for p in pairs}
    sol = {}
    for p in pairs:
        h, c = p
        rs = rows_of(c)
        beta = _sigmoid(small_ref[0, rs, SM_B + h:SM_B + h + 1])
        rhs = jnp.concatenate([v_ref[0, h, rs, :] * beta, kb[p] * jnp.exp(gcum_c[p])], axis=1)
        sol[p] = rhs + _dot(r[p], rhs)

    for c in range(n_chunks):
        rs = rows_of(c)
        state = {h: state_ref[h] for h in heads}
        v_new = {h: sol[(h, c)][:, :DN_DV] - _dot(sol[(h, c)][:, DN_DV:], state[h]) for h in heads}
        o_state = {h: _dot(q_ref[0, h, rs, :] * jnp.exp(gcum_c[(h, c)]), state[h]) for h in heads}
        o_local = {h: _dot(qk[(h, c)], v_new[h]) for h in heads}
        for h in heads:
            g = gcum_c[(h, c)]
            g_last = g[CHUNK - 1:CHUNK, :]
            k_tail = k_ref[0, h, rs, :] * jnp.exp(g_last - g)
            state_ref[h] = state[h] * jnp.exp(g_last) + _dot(k_tail, v_new[h], _TN)
        for h in heads:
            hc = slice(h * DN_DV, (h + 1) * DN_DV)
            o = o_state[h] + o_local[h]
            on = o * lax.rsqrt(jnp.mean(o * o, axis=1, keepdims=True) + RMS_EPS) * ng_ref[...]
            o_ref[0, rs, hc] = (on * _silu(z_ref[0, rs, hc])).astype(o_ref.dtype)


def _dn_chunk(q, k, v, small, a_row, z, a_log, dt_bias, norm_g, *, rows=256):
    bsz, heads, seq, dk = q.shape
    rows = min(rows, seq)
    hs = pl.BlockSpec((1, heads, rows, dk), lambda b, t, *_: (b, 0, t, 0))
    sms = pl.BlockSpec((1, rows, LANES), lambda b, t, *_: (b, t, 0))
    rsp = pl.BlockSpec((1, heads, 1, rows), lambda b, t, *_: (b, 0, 0, t))
    zs = pl.BlockSpec((1, rows, heads * DN_DV), lambda b, t, *_: (b, t, 0))
    return pl.pallas_call(
        functools.partial(_dnchunk_kernel, rows=rows),
        grid_spec=pltpu.PrefetchScalarGridSpec(
            num_scalar_prefetch=2,
            grid=(bsz, seq // rows),
            in_specs=[hs, hs, hs, sms, rsp, zs,
                      pl.BlockSpec((1, DN_DV), lambda b, t, *_: (0, 0))],
            out_specs=zs,
            scratch_shapes=[pltpu.VMEM((heads, DN_DK, DN_DV), f32)]),
        out_shape=jax.ShapeDtypeStruct((bsz, seq, heads * DN_DV), bf16),
        compiler_params=_cparams(("parallel", "arbitrary")),
        name="dn_chunk",
    )(a_log, dt_bias, q, k, v, small, a_row, z, norm_g.reshape(1, DN_DV))


def _layer_norm(r, g, b):
    mu = jnp.mean(r, axis=-1, keepdims=True)
    d = r - mu
    var = jnp.mean(d * d, axis=-1, keepdims=True)
    return d * lax.rsqrt(var + LN_EPS) * g + b


def _lane_min_index(hit, lane):
    return jnp.min(jnp.where(hit, lane, float(LANES)), axis=1, keepdims=True)


def _merge_kernel(ya_ref, yd_ref, ga_ref, gd_ref, x_ref, gt_ref, lng_ref, lnb_ref, sc_ref, sh_ref,
                  wa_ref, wd_ref, wo_ref, wr_ref, x1_ref, h2_ref, eidx_ref, gate_ref, *, alpha):
    merged = (ga_ref[0] * _dot(ya_ref[0], wa_ref[...]) + gd_ref[0] * _dot(yd_ref[0], wd_ref[...]))
    y = _dot(merged, wo_ref[...])
    x1 = _layer_norm(alpha * x_ref[0] + (1.0 + gt_ref[0]) * y, lng_ref[...], lnb_ref[...])
    x1_ref[0] = x1
    h2 = x1 * (1.0 + sc_ref[0]) + sh_ref[0]
    h2_ref[0] = h2
    logits = _dot(h2, wr_ref[...], exact=True)
    lane = lax.broadcasted_iota(i32, logits.shape, 1).astype(f32)
    lg = jnp.where(lane < N_GROUPS, logits, -jnp.inf)
    mg = jnp.max(lg, axis=1, keepdims=True)
    top_gp = 1.0 / jnp.sum(jnp.exp(lg - mg), axis=1, keepdims=True)
    g_idx = _lane_min_index(lg == mg, lane)
    lo = N_GROUPS + g_idx * EXPERTS_PER_GROUP
    in_grp = jnp.logical_and(lane >= lo, lane < lo + EXPERTS_PER_GROUP)
    le = jnp.where(in_grp, logits, -jnp.inf)
    m1 = jnp.max(le, axis=1, keepdims=True)
    i1 = _lane_min_index(le == m1, lane)
    le2 = jnp.where(lane == i1, -jnp.inf, le)
    m2 = jnp.max(le2, axis=1, keepdims=True)
    i2 = _lane_min_index(le2 == m2, lane)
    e2 = jnp.exp(m2 - m1)
    gate1 = top_gp / (1.0 + e2)
    gate2 = top_gp * e2 / (1.0 + e2)
    e_lanes = jnp.where(lane == 0.0, i1 - N_GROUPS, jnp.where(lane == 1.0, i2 - N_GROUPS, 0.0))
    eidx_ref[0] = e_lanes.astype(i32)
    gate_ref[0] = jnp.where(lane == 0.0, gate1, jnp.where(lane == 1.0, gate2, 0.0))


def _merge(y_att, y_dn, gates, x, gt1, ln_g, ln_b, sc2, sh2, w_br_att, w_br_dn, w_out, w_route,
           alpha, *, tm=256):
    bsz, seq, d = x.shape
    tm = min(tm, seq)
    row = lambda b, i: (b, i, 0)
    per_b = pl.BlockSpec((1, 1, d), lambda b, i: (b, 0, 0))
    vec = pl.BlockSpec((1, d), lambda b, i: (0, 0))
    wsp = pl.BlockSpec((d, d), lambda b, i: (0, 0))
    out_f = jax.ShapeDtypeStruct((bsz, seq, d), f32)
    return pl.pallas_call(
        functools.partial(_merge_kernel, alpha=alpha),
        grid=(bsz, seq // tm),
        in_specs=[pl.BlockSpec((1, tm, d), row), pl.BlockSpec((1, tm, d), row),
                  pl.BlockSpec((1, tm, d), lambda b, i: (b, i, 0)),
                  pl.BlockSpec((1, tm, d), lambda b, i: (b, i, 1)),
                  pl.BlockSpec((1, tm, d), row), per_b, vec, vec, per_b, per_b,
                  wsp, wsp, wsp, pl.BlockSpec((d, LANES), lambda b, i: (0, 0))],
        out_specs=[pl.BlockSpec((1, tm, d), row), pl.BlockSpec((1, tm, d), row),
                   pl.BlockSpec((1, tm, LANES), row), pl.BlockSpec((1, tm, LANES), row)],
        out_shape=[out_f, out_f, jax.ShapeDtypeStruct((bsz, seq, LANES), i32),
                   jax.ShapeDtypeStruct((bsz, seq, LANES), f32)],
        compiler_params=_cparams(("parallel", "parallel")),
        name="merge_router",
    )(y_att, y_dn, gates, gates, x, gt1, ln_g, ln_b, sc2, sh2, w_br_att, w_br_dn, w_out, w_route)


def _expert_kernel(blk_e_ref, tok_ref, slot_ref, h_hbm, wg_ref, wu_ref, wd_ref, y_hbm,
                   xbuf, ybuf, gsem, ssem):
    i = pl.program_id(0)
    nb = pl.num_programs(0)
    cur = i % 2

    def gather_copy(blk, buf, r):
        tok = tok_ref[blk * MOE_BLOCK + r]
        return pltpu.make_async_copy(h_hbm.at[pl.ds(tok, 1), :], xbuf.at[buf, pl.ds(r, 1), :],
                                     gsem.at[buf])

    def scatter_copy(blk, buf, r):
        slot = slot_ref[blk * MOE_BLOCK + r]
        return pltpu.make_async_copy(ybuf.at[buf, pl.ds(r, 1), :], y_hbm.at[pl.ds(slot, 1), :],
                                     ssem.at[buf])

    def start_rows(copy_fn, blk, buf):
        def body(r, carry):
            copy_fn(blk, buf, r).start()
            return carry
        lax.fori_loop(0, MOE_BLOCK, body, 0, unroll=8)

    def wait_gather(buf):
        pltpu.make_async_copy(h_hbm.at[pl.ds(0, MOE_BLOCK), :], xbuf.at[buf], gsem.at[buf]).wait()

    def wait_scatter(buf):
        pltpu.make_async_copy(ybuf.at[buf], y_hbm.at[pl.ds(0, MOE_BLOCK), :], ssem.at[buf]).wait()

    def start_rows_inline(copy_fn, blk, buf):
        for r in range(MOE_BLOCK):
            copy_fn(blk, buf, r).start()

    def ffn(buf):
        xb = xbuf[buf]
        hb = _silu(_dot(xb, wg_ref[0])) * _dot(xb, wu_ref[0])
        return _dot(hb, wd_ref[0])

    gcur = i % GATHER_SLOTS
    gnew = (i + 2) % GATHER_SLOTS

    @pl.when(i == 0)
    def _():
        start_rows(gather_copy, 0, 0)
        start_rows(gather_copy, 1, 1)

    wait_gather(gcur)

    @pl.when(i >= 2)
    def _():
        wait_scatter(cur)

    @pl.when(i == 0)
    def _():
        start_rows_inline(gather_copy, i + 2, gnew)
        ybuf[cur] = ffn(gcur)

    @pl.when(jnp.logical_and(i > 0, i < nb - 2))
    def _():
        start_rows_inline(gather_copy, i + 2, gnew)
        start_rows_inline(scatter_copy, i - 1, 1 - cur)
        ybuf[cur] = ffn(gcur)

    @pl.when(i >= nb - 2)
    def _():
        start_rows_inline(scatter_copy, i - 1, 1 - cur)
        ybuf[cur] = ffn(gcur)

    @pl.when(i == nb - 1)
    def _():
        start_rows(scatter_copy, i, cur)
        wait_scatter(1 - cur)
        wait_scatter(cur)


def _experts(h2, blk_e, tok_buf, slot_buf, w_gate, w_up, w_down, n_slots):
    n_tok, d = h2.shape
    nb = blk_e.shape[0]
    ff = w_gate.shape[-1]
    assert nb >= 4
    return pl.pallas_call(
        _expert_kernel,
        grid_spec=pltpu.PrefetchScalarGridSpec(
            num_scalar_prefetch=3,
            grid=(nb,),
            in_specs=[pl.BlockSpec(memory_space=pl.ANY),
                      pl.BlockSpec((1, d, ff), lambda i, be, *_: (be[i], 0, 0)),
                      pl.BlockSpec((1, d, ff), lambda i, be, *_: (be[i], 0, 0)),
                      pl.BlockSpec((1, ff, d), lambda i, be, *_: (be[i], 0, 0))],
            out_specs=pl.BlockSpec(memory_space=pl.ANY),
            scratch_shapes=[pltpu.VMEM((GATHER_SLOTS, MOE_BLOCK, d), f32),
                            pltpu.VMEM((2, MOE_BLOCK, d), f32),
                            pltpu.SemaphoreType.DMA((GATHER_SLOTS,)),
                            pltpu.SemaphoreType.DMA((2,))]),
        out_shape=jax.ShapeDtypeStruct((n_slots, d), f32),
        compiler_params=_cparams(("arbitrary",)),
        name="moe_experts",
    )(blk_e, tok_buf, slot_buf, h2, w_gate, w_up, w_down)


def _route_plan(e_idx, n_tok):
    m = n_tok * EXPERT_TOPK
    flat_e = e_idx.reshape(m)
    order = jnp.argsort(flat_e, stable=True).astype(i32)
    experts = jnp.arange(N_EXPERTS, dtype=i32)
    counts = jnp.sum((flat_e[:, None] == experts[None, :]).astype(i32), axis=0)
    pcounts = (counts + MOE_BLOCK - 1) // MOE_BLOCK * MOE_BLOCK
    starts = jnp.cumsum(counts) - counts
    pends = jnp.cumsum(pcounts)
    pstarts = pends - pcounts
    nb = -(-m // MOE_BLOCK) + N_EXPERTS
    rows = nb * MOE_BLOCK
    blk_raw = jnp.sum((pends[None, :] <= (jnp.arange(nb, dtype=i32) * MOE_BLOCK)[:, None]).astype(i32),
                      axis=1)
    blk_e = jnp.minimum(blk_raw, N_EXPERTS - 1)
    pos = jnp.arange(rows, dtype=i32)
    in_region = jnp.repeat(blk_raw, MOE_BLOCK) < N_EXPERTS
    e_row = jnp.repeat(blk_e, MOE_BLOCK)
    rank = pos - pstarts[e_row]
    valid = jnp.logical_and(in_region, rank < counts[e_row])
    assign = order[jnp.clip(starts[e_row] + rank, 0, m - 1)]
    n_before = jnp.where(in_region, starts[e_row] + jnp.minimum(rank, counts[e_row]), m)
    slot = (assign % EXPERT_TOPK) * n_tok + assign // EXPERT_TOPK
    slot_buf = jnp.where(valid, slot, m + pos - n_before)
    tok_buf = jnp.where(valid, assign // EXPERT_TOPK, 0)
    return blk_e, tok_buf, slot_buf, rows - m


def _final_kernel(y0_ref, y1_ref, g_ref, x_ref, gt_ref, lng_ref, lnb_ref, o_ref, *, alpha):
    y = y0_ref[...] * g_ref[0, :, 0:1] + y1_ref[...] * g_ref[0, :, 1:2]
    o_ref[0] = _layer_norm(alpha * x_ref[0] + (1.0 + gt_ref[0]) * y, lng_ref[...], lnb_ref[...])


def _final(y_slots, g_lanes, x, gt2, ln_g, ln_b, alpha, *, tm=512):
    bsz, seq, d = x.shape
    tm = min(tm, seq)
    nblk = seq // tm
    second = bsz * nblk
    per_b = pl.BlockSpec((1, 1, d), lambda b, i: (b, 0, 0))
    vec = pl.BlockSpec((1, d), lambda b, i: (0, 0))
    return pl.pallas_call(
        functools.partial(_final_kernel, alpha=alpha),
        grid=(bsz, nblk),
        in_specs=[pl.BlockSpec((tm, d), lambda b, i: (b * nblk + i, 0)),
                  pl.BlockSpec((tm, d), lambda b, i: (second + b * nblk + i, 0)),
                  pl.BlockSpec((1, tm, LANES), lambda b, i: (b, i, 0)),
                  pl.BlockSpec((1, tm, d), lambda b, i: (b, i, 0)), per_b, vec, vec],
        out_specs=pl.BlockSpec((1, tm, d), lambda b, i: (b, i, 0)),
        out_shape=jax.ShapeDtypeStruct((bsz, seq, d), f32),
        compiler_params=_cparams(("parallel", "parallel")),
        name="moe_combine_ln",
    )(y_slots, y_slots, g_lanes, x, gt2, ln_g, ln_b)


def _pack_w_in(w_in):
    sizes = (ATT_HEADS * ATT_HEAD_DIM, KV_LATENT, IDX_HEADS * IDX_DIM, IDX_DIM, IDX_HEADS,
             3 * DN_HEADS * DN_DK, DN_HEADS, DN_HEADS, DN_HEADS * DN_DV, 2 * w_in.shape[0])
    offs = [0]
    for s in sizes:
        offs.append(offs[-1] + s)
    w_in = w_in.astype(bf16)
    seg = [w_in[:, offs[j]:offs[j + 1]] for j in range(len(sizes))]
    w_q, w_ckv, w_qidx, w_kidx, w_widx, w_qkv, w_a, w_b, w_z, w_gates = seg
    pad = jnp.zeros((w_in.shape[0], LANES - (IDX_DIM + IDX_HEADS + 2 * DN_HEADS)), w_in.dtype)
    packed = jnp.concatenate([w_q, w_ckv, w_qidx, w_kidx, w_widx, w_a, w_b, pad, w_qkv, w_z, w_gates],
                             axis=1)
    return packed


def kernel(x, c, w_ada, b_ada, w_in, kv_norm_g, w_uk, w_uv, conv_w, a_log, dt_bias, dn_norm_g,
           w_br_att, w_br_dn, w_out, w_route_grp, w_route_exp, w_gate, w_up, w_down, ln_g, ln_b):
    depth = w_in.shape[0]
    bsz, seq, d = x.shape
    n_tok = bsz * seq
    alpha = (2.0 * depth) ** 0.25
    mod = _ada(c, w_ada, b_ada)
    for l in range(depth):
        sh1, sc1, gt1, sh2, sc2, gt2 = [mod[l, :, j * d:(j + 1) * d].reshape(bsz, 1, d)
                                        for j in range(6)]
        q_lat, kv, q_idx, small, qkv, z, gates, k_idx = _proj(
            x, sc1, sh1, _pack_w_in(w_in[l]), jnp.swapaxes(w_uk[l], 1, 2).astype(bf16), kv_norm_g[l])

        y_att = _dsa_attention(q_lat, q_idx, small, k_idx, kv, w_uv[l].astype(bf16))

        dq, dk, dv = _dn_prep(qkv, conv_w[l])
        a_row = jnp.swapaxes(small[..., SM_A:SM_A + DN_HEADS], 1, 2)[:, :, None, :]
        y_dn = _dn_chunk(dq, dk, dv, small, a_row, z, a_log[l], dt_bias[l], dn_norm_g[l])

        w_route = jnp.zeros((d, LANES), f32)
        w_route = w_route.at[:, :N_GROUPS].set(w_route_grp[l])
        w_route = w_route.at[:, N_GROUPS:N_GROUPS + N_EXPERTS].set(w_route_exp[l])
        x1, h2, e_lanes, g_lanes = _merge(
            y_att, y_dn, gates, x, gt1, ln_g[l, 0].reshape(1, d), ln_b[l, 0].reshape(1, d),
            sc2, sh2, w_br_att[l].astype(bf16), w_br_dn[l].astype(bf16), w_out[l].astype(bf16),
            w_route, alpha)

        e_idx = e_lanes.reshape(n_tok, LANES)[:, :EXPERT_TOPK]
        blk_e, tok_buf, slot_buf, n_pad = _route_plan(e_idx, n_tok)
        y_slots = _experts(h2.reshape(n_tok, d), blk_e, tok_buf, slot_buf,
                           w_gate[l], w_up[l], w_down[l],
                           n_tok * EXPERT_TOPK + n_pad)
        x = _final(y_slots, g_lanes, x1, gt2, ln_g[l, 1].reshape(1, d), ln_b[l, 1].reshape(1, d), alpha)
    return x
```
